```python
import math
import jax, jax.numpy as jnp
from jax import lax
import numpy as np

D_MODEL = 1024
BATCH = 16
SEQ = 2048
DEPTH = 2

CTX_LEN = 256
GRID_W = 64
D_MIX = 1024
EPS = 1e-6
ROPE_BASE = 10000.0
Q_BLOCK = 128

MLA_HEADS = 6
MLA_NOPE = 64
MLA_ROPE = 32
MLA_V = 64
MLA_Q_RANK = 384
MLA_KV_RANK = 256
MLA_WIDTH = MLA_HEADS * MLA_V
MLA_SCALE = (MLA_NOPE + MLA_ROPE) ** -0.5

GQA_HEADS = 6
GQA_KV_HEADS = 2
GQA_GROUP = GQA_HEADS // GQA_KV_HEADS
GQA_DIM = 64
GQA_WIDTH = GQA_HEADS * GQA_DIM
GQA_SCALE = GQA_DIM ** -0.5

GDN_HEADS = 4
GDN_DK = 64
GDN_DV = 64
GDN_WIDTH = GDN_HEADS * GDN_DV
GDN_CONV = 5
GDN_CHUNK = 64

IN_SPLITS = (
    MLA_Q_RANK,
    MLA_KV_RANK,
    MLA_ROPE,
    GQA_HEADS * GQA_DIM,
    GQA_KV_HEADS * GQA_DIM,
    GQA_KV_HEADS * GQA_DIM,
    3 * GDN_HEADS * GDN_DK,
    2 * GDN_HEADS,
    2 * GDN_HEADS,
    D_MIX,
)
D_IN = sum(IN_SPLITS)

kernel_name = 'hybrid_mla_gqa_gdn_dit_block'


def rms_norm(x, g):
    xf = x.astype(jnp.float32)
    y = xf * lax.rsqrt(jnp.mean(xf * xf, axis=-1, keepdims=True) + EPS)
    return (y * g.astype(jnp.float32)).astype(x.dtype)


def l2norm(x):
    return x * lax.rsqrt(jnp.sum(x * x, axis=-1, keepdims=True) + EPS)


def split_cols(h):
    parts, start = [], 0
    for n in IN_SPLITS:
        parts.append(h[..., start:start + n])
        start += n
    return parts


def rope_1d(x, pos):
    d = x.shape[-1]
    half = d // 2
    inv = ROPE_BASE ** (-jnp.arange(0, d, 2, dtype=jnp.float32) / d)
    ang = pos.astype(jnp.float32)[:, None] * inv[None, :]
    shape = (pos.shape[0],) + (1,) * (x.ndim - 3) + (half,)
    cos, sin = jnp.cos(ang).reshape(shape), jnp.sin(ang).reshape(shape)
    xf = x.astype(jnp.float32)
    x1, x2 = xf[..., :half], xf[..., half:]
    return jnp.concatenate([x1 * cos - x2 * sin, x2 * cos + x1 * sin], -1).astype(x.dtype)


def rope_2d(x, rows, cols):
    h = x.shape[-1] // 2
    return jnp.concatenate([rope_1d(x[..., :h], rows), rope_1d(x[..., h:], cols)], -1)


def attend(q, k, v, scale):
    B, Lq, Hk, G, dq = q.shape
    nb = Lq // Q_BLOCK
    qb = jnp.moveaxis(q.reshape(B, nb, Q_BLOCK, Hk, G, dq), 1, 0)

    def one_block(qi):
        s = jnp.einsum('bqhgd,bkhd->bhgqk', qi, k).astype(jnp.float32) * scale
        p = jax.nn.softmax(s, axis=-1).astype(v.dtype)
        return jnp.einsum('bhgqk,bkhe->bqhge', p, v)

    o = lax.map(one_block, qb)
    return jnp.moveaxis(o, 0, 1).reshape(B, Lq, Hk * G * v.shape[-1])


def mla_q(cq, qn_g, w_uq, pos):
    B, L, _ = cq.shape
    q = (rms_norm(cq, qn_g) @ w_uq).reshape(B, L, MLA_HEADS, MLA_NOPE + MLA_ROPE)
    q_nope, q_rope = q[..., :MLA_NOPE], q[..., MLA_NOPE:]
    if pos is not None:
        q_rope = rope_2d(q_rope, *pos)
    return jnp.concatenate([q_nope, q_rope], -1)[:, :, :, None, :]


def mla_kv(ckv, kr, kvn_g, w_ukv, pos):
    B, L, _ = ckv.shape
    kv = (rms_norm(ckv, kvn_g) @ w_ukv).reshape(B, L, MLA_HEADS, MLA_NOPE + MLA_V)
    k_nope, v = kv[..., :MLA_NOPE], kv[..., MLA_NOPE:]
    kr = kr[:, :, None, :]
    if pos is not None:
        kr = rope_2d(kr, *pos)
    k = jnp.concatenate([k_nope, jnp.broadcast_to(kr, (B, L, MLA_HEADS, MLA_ROPE))], -1)
    return k, v


def gqa_q(q, qn_g, pos):
    B, L, _ = q.shape
    q = rms_norm(q.reshape(B, L, GQA_KV_HEADS, GQA_GROUP, GQA_DIM), qn_g)
    return rope_2d(q, *pos) if pos is not None else q


def gqa_kv(k, v, kn_g, pos):
    B, L, _ = k.shape
    k = rms_norm(k.reshape(B, L, GQA_KV_HEADS, GQA_DIM), kn_g)
    k = rope_2d(k, *pos) if pos is not None else k
    return k, v.reshape(B, L, GQA_KV_HEADS, GQA_DIM)


def short_conv(x, w):
    C = x.shape[-1]
    return lax.conv_general_dilated(
        x, w[:, None, :].astype(x.dtype), window_strides=(1,),
        padding=[(GDN_CONV // 2, GDN_CONV // 2)],
        dimension_numbers=('NWC', 'WIO', 'NWC'), feature_group_count=C)


def gdn_prep(qkv, b, a, conv_w, a_log, dt_bias):
    B, L, _ = qkv.shape
    y = jax.nn.silu(short_conv(qkv, conv_w)).astype(jnp.float32)
    q, k, v = jnp.split(y, 3, axis=-1)
    q = l2norm(q.reshape(B, L, GDN_HEADS, GDN_DK))
    k = l2norm(k.reshape(B, L, GDN_HEADS, GDN_DK))
    v = v.reshape(B, L, GDN_HEADS, GDN_DV)
    beta = jax.nn.sigmoid(b.astype(jnp.float32)).reshape(B, L, 2, GDN_HEADS)
    g = -jnp.exp(a_log.astype(jnp.float32)) * jax.nn.softplus(
        a.astype(jnp.float32).reshape(B, L, 2, GDN_HEADS) + dt_bias.astype(jnp.float32))
    return q, k, v, beta, g


def gated_delta_chunked(q, k, v, g, beta, s0):
    B, L, H, DK = q.shape
    DV = v.shape[-1]
    C = GDN_CHUNK
    N = L // C
    to_chunks = lambda t: jnp.moveaxis(t.reshape(B, N, C, H, *t.shape[3:]), 3, 1)
    q = to_chunks(q) * DK ** -0.5
    k = to_chunks(k)
    v = to_chunks(v)
    beta = to_chunks(beta)
    gc = jnp.cumsum(to_chunks(g), axis=-1)
    idx = jnp.arange(C)
    incl = idx[:, None] >= idx[None, :]
    strict = idx[:, None] > idx[None, :]
    decay = jnp.exp(jnp.where(incl, gc[..., :, None] - gc[..., None, :], -jnp.inf))
    kb = k * beta[..., None]
    kk = jnp.where(strict, jnp.einsum('bhnid,bhnjd->bhnij', kb, k) * decay, 0.0)
    rhs = jnp.concatenate([v * beta[..., None], kb * jnp.exp(gc)[..., None]], -1)
    sol = lax.linalg.triangular_solve(jnp.eye(C, dtype=q.dtype) + kk, rhs,
                                      left_side=True, lower=True)
    u, w = sol[..., :DV], sol[..., DV:]
    a_intra = jnp.einsum('bhnid,bhnjd->bhnij', q, k) * decay
    q_dec = q * jnp.exp(gc)[..., None]
    k_dec = k * jnp.exp(gc[..., -1:] - gc)[..., None]
    g_last = jnp.exp(gc[..., -1])

    def step(S, xs):
        u_i, w_i, a_i, qd_i, kd_i, gl_i = xs
        v_new = u_i - jnp.einsum('bhck,bhkv->bhcv', w_i, S)
        o_i = (jnp.einsum('bhck,bhkv->bhcv', qd_i, S)
               + jnp.einsum('bhcj,bhjv->bhcv', a_i, v_new))
        S = S * gl_i[..., None, None] + jnp.einsum('bhck,bhcv->bhkv', kd_i, v_new)
        return S, o_i

    xs = tuple(jnp.moveaxis(t, 2, 0) for t in (u, w, a_intra, q_dec, k_dec, g_last))
    S, o = lax.scan(step, s0, xs)
    o = jnp.moveaxis(jnp.moveaxis(o, 0, 2), 1, 3).reshape(B, L, H, DV)
    return o, S


def gdn_bidirectional(lat, ctx):
    ql, kl, vl, bl, gl = lat
    qc, kc, vc, bc, gcx = ctx
    s0 = jnp.zeros((qc.shape[0], GDN_HEADS, GDN_DK, GDN_DV), jnp.float32)
    o_l, o_c = None, None
    for d in range(2):
        fl = (lambda t: jnp.flip(t, axis=1)) if d == 1 else (lambda t: t)
        oc_d, s_ctx = gated_delta_chunked(fl(qc), fl(kc), fl(vc), fl(gcx[:, :, d]),
                                          fl(bc[:, :, d]), s0)
        ol_d, _ = gated_delta_chunked(fl(ql), fl(kl), fl(vl), fl(gl[:, :, d]),
                                      fl(bl[:, :, d]), s_ctx)
        o_l = fl(ol_d) if o_l is None else o_l + fl(ol_d)
        o_c = fl(oc_d) if o_c is None else o_c + fl(oc_d)
    return o_l, o_c


def merge_branches(o_mla, o_gqa, o_gdn, gate, on_g, w_out, dtype):
    B, L = o_gdn.shape[:2]
    o_gdn = rms_norm(o_gdn, on_g).reshape(B, L, GDN_WIDTH)
    o = jnp.concatenate([o_mla.astype(dtype), o_gqa.astype(dtype), o_gdn.astype(dtype)], -1)
    return (o * jax.nn.silu(gate)) @ w_out


def hybrid_layer(x, ctx, mod_lat, mod_ctx, norm_g, w_in, mla_qn_g, mla_w_uq, mla_kvn_g,
                 mla_w_ukv, gqa_qn_g, gqa_kn_g, gdn_conv_w, gdn_a_log, gdn_dt_bias,
                 gdn_on_g, w_out, pos, update_ctx):
    sh_l, sc_l, gt_l = jnp.split(mod_lat, 3, axis=-1)
    sh_c, sc_c, gt_c = jnp.split(mod_ctx, 3, axis=-1)
    h_l = rms_norm(x, norm_g) * (1 + sc_l) + sh_l
    h_c = rms_norm(ctx, norm_g) * (1 + sc_c) + sh_c
    (l_cq, l_ckv, l_kr, l_q, l_k, l_v, l_gdn, l_b, l_a, l_gate) = split_cols(h_l @ w_in)
    (c_cq, c_ckv, c_kr, c_q, c_k, c_v, c_gdn, c_b, c_a, c_gate) = split_cols(h_c @ w_in)

    kc_a, vc_a = mla_kv(c_ckv, c_kr, mla_kvn_g, mla_w_ukv, None)
    kl_a, vl_a = mla_kv(l_ckv, l_kr, mla_kvn_g, mla_w_ukv, pos)
    o_mla_l = attend(mla_q(l_cq, mla_qn_g, mla_w_uq, pos),
                     jnp.concatenate([kl_a, kc_a], 1), jnp.concatenate([vl_a, vc_a], 1), MLA_SCALE)

    kc_b, vc_b = gqa_kv(c_k, c_v, gqa_kn_g, None)
    kl_b, vl_b = gqa_kv(l_k, l_v, gqa_kn_g, pos)
    o_gqa_l = attend(gqa_q(l_q, gqa_qn_g, pos),
                     jnp.concatenate([kl_b, kc_b], 1), jnp.concatenate([vl_b, vc_b], 1), GQA_SCALE)

    o_gdn_l, o_gdn_c = gdn_bidirectional(
        gdn_prep(l_gdn, l_b, l_a, gdn_conv_w, gdn_a_log, gdn_dt_bias),
        gdn_prep(c_gdn, c_b, c_a, gdn_conv_w, gdn_a_log, gdn_dt_bias))

    x_new = x + gt_l * merge_branches(o_mla_l, o_gqa_l, o_gdn_l, l_gate, gdn_on_g, w_out, x.dtype)
    if update_ctx:
        o_mla_c = attend(mla_q(c_cq, mla_qn_g, mla_w_uq, None), kc_a, vc_a, MLA_SCALE)
        o_gqa_c = attend(gqa_q(c_q, gqa_qn_g, None), kc_b, vc_b, GQA_SCALE)
        ctx = ctx + gt_c * merge_branches(o_mla_c, o_gqa_c, o_gdn_c, c_gate, gdn_on_g, w_out, ctx.dtype)
    return x_new, ctx


def setup_inputs(seed: int = 0) -> dict:
    key = jax.random.key(seed)
    ks = jax.random.split(key, 24)
    f32 = jnp.float32
    nrm = lambda k, shape, s: s * jax.random.normal(k, shape, f32)
    gain = lambda k, shape: 1.0 + 0.02 * jax.random.normal(k, shape, f32)
    dt = jnp.exp(jax.random.uniform(ks[16], (DEPTH, 2, GDN_HEADS), f32,
                                    minval=math.log(1e-3), maxval=math.log(1e-1)))
    return {
        'x': nrm(ks[0], (BATCH, SEQ, D_MODEL), 1.0),
        'c': nrm(ks[1], (BATCH, D_MODEL), 1.0),
        'ctx': nrm(ks[2], (BATCH, CTX_LEN, D_MODEL), 1.0),
        'c_ctx': nrm(ks[3], (D_MODEL,), 1.0),
        'w_ada': nrm(ks[4], (DEPTH, D_MODEL, 3 * D_MODEL), 0.5 * D_MODEL ** -0.5),
        'b_ada': nrm(ks[5], (DEPTH, 3 * D_MODEL), 0.02),
        'norm_g': gain(ks[6], (DEPTH, D_MODEL)),
        'w_in': nrm(ks[7], (DEPTH, D_MODEL, D_IN), D_MODEL ** -0.5),
        'mla_qn_g': gain(ks[8], (DEPTH, MLA_Q_RANK)),
        'mla_w_uq': nrm(ks[9], (DEPTH, MLA_Q_RANK, MLA_HEADS * (MLA_NOPE + MLA_ROPE)), MLA_Q_RANK ** -0.5),
        'mla_kvn_g': gain(ks[10], (DEPTH, MLA_KV_RANK)),
        'mla_w_ukv': nrm(ks[11], (DEPTH, MLA_KV_RANK, MLA_HEADS * (MLA_NOPE + MLA_V)), MLA_KV_RANK ** -0.5),
        'gqa_qn_g': gain(ks[12], (DEPTH, GQA_DIM)),
        'gqa_kn_g': gain(ks[13], (DEPTH, GQA_DIM)),
        'gdn_conv_w': nrm(ks[14], (DEPTH, GDN_CONV, 3 * GDN_HEADS * GDN_DK), GDN_CONV ** -0.5),
        'gdn_a_log': jnp.log(jax.random.uniform(ks[15], (DEPTH, 2, GDN_HEADS), f32, minval=1.0, maxval=16.0)),
        'gdn_dt_bias': dt + jnp.log(-jnp.expm1(-dt)),
        'gdn_on_g': gain(ks[17], (DEPTH, GDN_DV)),
        'w_out': nrm(ks[18], (DEPTH, D_MIX, D_MODEL), D_MIX ** -0.5),
        'final_g': gain(ks[19], (D_MODEL,)),
    }


def reference(x, c, ctx, c_ctx, w_ada, b_ada, norm_g, w_in, mla_qn_g, mla_w_uq, mla_kvn_g,
              mla_w_ukv, gqa_qn_g, gqa_kn_g, gdn_conv_w, gdn_a_log, gdn_dt_bias, gdn_on_g,
              w_out, final_g):
    seq_len = x.shape[1]
    ROWS = seq_len // GRID_W
    rows = jnp.repeat(jnp.arange(ROWS, dtype=jnp.int32), GRID_W)
    cols = jnp.tile(jnp.arange(GRID_W, dtype=jnp.int32), ROWS)
    pos = (rows, cols)
    s_c = jax.nn.silu(c)
    s_cc = jax.nn.silu(c_ctx)
    for l in range(DEPTH):
        mod_lat = (s_c @ w_ada[l] + b_ada[l])[:, None, :]
        mod_ctx = s_cc @ w_ada[l] + b_ada[l]
        x, ctx = hybrid_layer(x, ctx, mod_lat, mod_ctx, norm_g[l], w_in[l], mla_qn_g[l],
                              mla_w_uq[l], mla_kvn_g[l], mla_w_ukv[l], gqa_qn_g[l], gqa_kn_g[l],
                              gdn_conv_w[l], gdn_a_log[l], gdn_dt_bias[l], gdn_on_g[l], w_out[l],
                              pos, l < DEPTH - 1)
    return rms_norm(x, final_g)
```

```python
import functools
import math

import numpy as np
import jax
import jax.numpy as jnp
from jax import lax
from jax.experimental import pallas as pl
from jax.experimental.pallas import tpu as pltpu

F32 = jnp.float32
BF16 = jnp.bfloat16
HIGHEST = lax.Precision.HIGHEST

D_MODEL = 1024
DEPTH = 2
CTX_LEN = 256
GRID_W = 64
D_MIX = 1024
EPS = 1e-6
ROPE_BASE = 10000.0

MLA_HEADS = 6
MLA_NOPE = 64
MLA_ROPE = 32
MLA_V = 64
MLA_Q_RANK = 384
MLA_KV_RANK = 256
MLA_SCALE = (MLA_NOPE + MLA_ROPE) ** -0.5

GQA_HEADS = 6
GQA_KV_HEADS = 2
GQA_DIM = 64
GQA_SCALE = GQA_DIM ** -0.5

GDN_HEADS = 4
GDN_DK = 64
GDN_DV = 64
GDN_WIDTH = GDN_HEADS * GDN_DV
GDN_CONV = 5
GDN_CHUNK = 64

LANE = 128
SUBLANE = 8
TOK_TILE = 256
VMEM_LIMIT = 48 * 1024 * 1024

_W_CQ = MLA_Q_RANK
_W_CKV = MLA_KV_RANK
_W_KR = LANE
_W_GQ = GQA_HEADS * GQA_DIM
_W_GK3 = 3 * LANE
_W_GV3 = 3 * LANE
_W_GDN = 3 * GDN_WIDTH
_W_BX = 2 * GDN_WIDTH
_W_AX = 2 * GDN_WIDTH
_W_GATE = D_MIX
_OFFS = np.cumsum([0, _W_CQ, _W_CKV, _W_KR, _W_GQ, _W_GK3, _W_GV3, _W_GDN, _W_BX, _W_AX, _W_GATE])
(_O_CQ, _O_CKV, _O_KR, _O_GQ, _O_GK3, _O_GV3, _O_GDN, _O_BX, _O_AX, _O_GATE, D_IN_P) = [int(v) for v in _OFFS]


def _dot(a, b):
    return jnp.dot(a, b, preferred_element_type=F32)


def _dot_nt(a, b):
    return lax.dot_general(a, b, (((1,), (1,)), ((), ())), preferred_element_type=F32)


def _params(*sem):
    return pltpu.CompilerParams(dimension_semantics=sem, vmem_limit_bytes=VMEM_LIMIT)


def _adaln_kernel(c_ref, w_ref, b_ref, o_ref):
    c = c_ref[...]
    s = c * jax.nn.sigmoid(c)
    o_ref[0] = jnp.dot(s, w_ref[0], precision=HIGHEST, preferred_element_type=F32) + b_ref[0]


def _adaln(cvec, w_ada, b_ada):
    rows = cvec.shape[0]
    nblk = w_ada.shape[2] // D_MODEL
    return pl.pallas_call(
        _adaln_kernel,
        grid=(DEPTH, nblk),
        in_specs=[
            pl.BlockSpec((rows, D_MODEL), lambda l, j: (0, 0)),
            pl.BlockSpec((1, D_MODEL, D_MODEL), lambda l, j: (l, 0, j)),
            pl.BlockSpec((1, 1, D_MODEL), lambda l, j: (l, 0, j)),
        ],
        out_specs=pl.BlockSpec((1, rows, D_MODEL), lambda l, j: (l, 0, j)),
        out_shape=jax.ShapeDtypeStruct((DEPTH, rows, 3 * D_MODEL), F32),
        compiler_params=_params("arbitrary", "arbitrary"),
        name="adaln",
    )(cvec, w_ada, b_ada.reshape(DEPTH, 1, 3 * D_MODEL))


def _group_sum(sq, gsum):
    hi = sq.astype(BF16)
    lo = (sq - hi.astype(F32)).astype(BF16)
    return _dot(hi, gsum) + _dot(lo, gsum)


def _rope_block(xb, c, s, half):
    lane = lax.broadcasted_iota(jnp.int32, xb.shape, 1)
    up = pltpu.roll(xb, LANE - half, 1)
    dn = pltpu.roll(xb, half, 1)
    partner = jnp.where((lane % (2 * half)) < half, up, dn)
    return xb * c + partner * s


def _inproj_kernel(nt_lat, x_ref, xp_ref, xn_ref, ctx_ref, shl_ref, scl_ref, shc_ref, scc_ref,
                   ng_ref, win_ref, qng_ref, wuq_ref, kvng_ref, wuk_ref, wuv_ref,
                   gqg_ref, gkg_ref, cw_ref, alog_ref, dtb_ref, ca_ref, sa_ref, cb_ref, sb_ref,
                   gsum_ref,
                   qa_ref, ka_ref, va_ref, qb_ref, kb_ref, vb_ref, qkv_ref, bgx_ref, sg_ref):
    t = pl.program_id(1)
    is_ctx = t == nt_lat
    tm = TOK_TILE
    xt = jnp.where(is_ctx, ctx_ref[0], x_ref[0])
    xe = jnp.concatenate([xp_ref[0], xt, xn_ref[0]], axis=0)
    sh = jnp.where(is_ctx, shc_ref[...], shl_ref[0])
    sc = jnp.where(is_ctx, scc_ref[...], scl_ref[0])
    ms = jnp.mean(xe * xe, axis=-1, keepdims=True)
    he = (xe * lax.rsqrt(ms + EPS) * ng_ref[...]) * (1.0 + sc) + sh
    hbe = he.astype(BF16)
    hb = he[SUBLANE:SUBLANE + tm].astype(BF16)

    ca, sa = ca_ref[...], sa_ref[...]
    cb, sb = cb_ref[...], sb_ref[...]

    zcq = _dot(hb, win_ref[:, _O_CQ:_O_CQ + _W_CQ])
    cqn = zcq * lax.rsqrt(jnp.mean(zcq * zcq, axis=-1, keepdims=True) + EPS) * qng_ref[...]
    q = _dot(cqn.astype(BF16), wuq_ref[...]) * MLA_SCALE
    for h in range(MLA_HEADS):
        sl = slice(LANE * h, LANE * (h + 1))
        qa_ref[0, :, sl] = _rope_block(q[:, sl], ca, sa, MLA_ROPE // 4).astype(BF16)

    zckv = _dot(hb, win_ref[:, _O_CKV:_O_CKV + _W_CKV])
    ckvn = (zckv * lax.rsqrt(jnp.mean(zckv * zckv, axis=-1, keepdims=True) + EPS) * kvng_ref[...]).astype(BF16)
    kr = _rope_block(_dot(hb, win_ref[:, _O_KR:_O_KR + _W_KR]), ca, sa, MLA_ROPE // 4)
    kn = _dot(ckvn, wuk_ref[...])
    for h in range(MLA_HEADS):
        sl = slice(LANE * h, LANE * (h + 1))
        ka_ref[0, :, sl] = (kn[:, sl] + kr).astype(BF16)
    va_ref[0] = _dot(ckvn, wuv_ref[...]).astype(BF16)

    gsum = gsum_ref[...]
    zq = _dot(hb, win_ref[:, _O_GQ:_O_GQ + _W_GQ])
    qn = zq * lax.rsqrt(_group_sum(zq * zq, gsum) * (1.0 / GQA_DIM) + EPS) * gqg_ref[...]
    lane = lax.broadcasted_iota(jnp.int32, (tm, LANE), 1)
    for p in range(GQA_HEADS // 2):
        blk = _rope_block(qn[:, LANE * p:LANE * (p + 1)], cb, sb, GQA_DIM // 4)
        qb_ref[0, :, LANE * (2 * p):LANE * (2 * p + 1)] = jnp.where(lane < GQA_DIM, blk, 0.0).astype(BF16)
        qb_ref[0, :, LANE * (2 * p + 1):LANE * (2 * p + 2)] = jnp.where(lane >= GQA_DIM, blk, 0.0).astype(BF16)
    zk = _dot(hb, win_ref[:, _O_GK3:_O_GK3 + _W_GK3])
    kn3 = zk * lax.rsqrt(_group_sum(zk * zk, gsum) * (1.0 / GQA_DIM) + EPS) * gkg_ref[...]
    for p in range(3):
        sl = slice(LANE * p, LANE * (p + 1))
        kb_ref[0, :, sl] = _rope_block(kn3[:, sl], cb, sb, GQA_DIM // 4).astype(BF16)
    vb_ref[0] = _dot(hb, win_ref[:, _O_GV3:_O_GV3 + _W_GV3]).astype(BF16)

    zg = _dot(hbe, win_ref[:, _O_GDN:_O_GDN + _W_GDN])
    row = lax.broadcasted_iota(jnp.int32, (tm + 2 * SUBLANE, 1), 0)
    has_prev = jnp.logical_and(t > 0, jnp.logical_not(is_ctx))
    has_next = t < nt_lat - 1
    keep = jnp.logical_and(jnp.logical_or(row >= SUBLANE, has_prev),
                           jnp.logical_or(row < SUBLANE + tm, has_next))
    zg = jnp.where(keep, zg, 0.0)
    pad = GDN_CONV // 2
    y = jnp.zeros((tm, _W_GDN), F32)
    for j in range(GDN_CONV):
        o = SUBLANE - pad + j
        y = y + zg[o:o + tm] * cw_ref[j:j + 1, :]
    y = y * jax.nn.sigmoid(y)
    gs = gsum[:GDN_WIDTH, :GDN_WIDTH]
    qg = y[:, :GDN_WIDTH]
    kg = y[:, GDN_WIDTH:2 * GDN_WIDTH]
    qkv_ref[0, :, 0:GDN_WIDTH] = qg * lax.rsqrt(_group_sum(qg * qg, gs) + EPS) * (GDN_DK ** -0.5)
    qkv_ref[0, :, GDN_WIDTH:2 * GDN_WIDTH] = kg * lax.rsqrt(_group_sum(kg * kg, gs) + EPS)
    qkv_ref[0, :, 2 * GDN_WIDTH:] = y[:, 2 * GDN_WIDTH:]
    zb = _dot(hb, win_ref[:, _O_BX:_O_BX + _W_BX])
    bgx_ref[0, :, 0:_W_BX] = jax.nn.sigmoid(zb)
    za = _dot(hb, win_ref[:, _O_AX:_O_AX + _W_AX]) + dtb_ref[...]
    softplus = jnp.maximum(za, 0.0) + jnp.log(1.0 + jnp.exp(-jnp.abs(za)))
    bgx_ref[0, :, _W_BX:] = -jnp.exp(alog_ref[...]) * softplus

    zgate = _dot(hb, win_ref[:, _O_GATE:_O_GATE + _W_GATE])
    sg_ref[0] = (zgate * jax.nn.sigmoid(zgate)).astype(BF16)


def _inproj(x, ctx, shl, scl, shc, scc, lw, tabs):
    B, L, D = x.shape
    nt_lat = L // TOK_TILE
    nt = nt_lat + CTX_LEN // TOK_TILE
    LT = L + CTX_LEN
    rb = TOK_TILE // SUBLANE
    nrb = L // SUBLANE
    full = lambda a: pl.BlockSpec(a.shape, lambda b, t: (0,) * a.ndim)
    tok = lambda w: pl.BlockSpec((1, TOK_TILE, w), lambda b, t: (b, t, 0))
    tab = pl.BlockSpec((TOK_TILE, LANE), lambda b, t: (t, 0))
    mod = pl.BlockSpec((1, 1, D), lambda b, t: (b, 0, 0))
    consts = [lw["norm_g"], lw["w_in"], lw["qn_g"], lw["w_uq"], lw["kvn_g"], lw["w_uk"], lw["w_uv"],
              lw["gq_g"], lw["gk_g"], lw["conv_w"], lw["alog_x"], lw["dtb_x"]]
    in_specs = [
        pl.BlockSpec((1, TOK_TILE, D), lambda b, t: (b, jnp.minimum(t, nt_lat - 1), 0)),
        pl.BlockSpec((1, SUBLANE, D), lambda b, t: (b, jnp.clip(t * rb - 1, 0, nrb - 1), 0)),
        pl.BlockSpec((1, SUBLANE, D), lambda b, t: (b, jnp.clip((t + 1) * rb, 0, nrb - 1), 0)),
        pl.BlockSpec((1, CTX_LEN, D), lambda b, t: (b, 0, 0)),
        mod, mod, full(shc), full(scc),
    ] + [full(a) for a in consts] + [tab, tab, tab, tab, full(tabs["gsum"])]
    out_w = [(MLA_HEADS * LANE, BF16), (MLA_HEADS * LANE, BF16), (MLA_HEADS * MLA_V, BF16),
             (GQA_HEADS * LANE, BF16), (_W_GK3, BF16), (_W_GV3, BF16),
             (_W_GDN, F32), (_W_BX + _W_AX, F32), (D_MIX, BF16)]
    return pl.pallas_call(
        functools.partial(_inproj_kernel, nt_lat),
        grid=(B, nt),
        in_specs=in_specs,
        out_specs=[tok(w) for w, _ in out_w],
        out_shape=[jax.ShapeDtypeStruct((B, LT, w), dt) for w, dt in out_w],
        compiler_params=_params("parallel", "arbitrary"),
        name="inproj",
    )(x, x, x, ctx, shl, scl, shc, scc, *consts,
      tabs["ca"], tabs["sa"], tabs["cb"], tabs["sb"], tabs["gsum"])


def _attn_head(q, k, vpair):
    s = _dot_nt(q, k)
    m = jnp.max(s, axis=-1, keepdims=True)
    p = jnp.exp(s - m)
    l = jnp.sum(p, axis=-1, keepdims=True)
    return _dot(p.astype(BF16), vpair) / l


def _attn_kernel(qa_ref, ka_ref, va_ref, qb_ref, kb_ref, vb_ref, o_ref):
    lane = lax.broadcasted_iota(jnp.int32, (qa_ref.shape[1], LANE), 1)
    lo = lane < MLA_V
    for p in range(MLA_HEADS // 2):
        v = va_ref[0, :, LANE * p:LANE * (p + 1)]
        o0 = _attn_head(qa_ref[0, :, LANE * (2 * p):LANE * (2 * p + 1)],
                        ka_ref[0, :, LANE * (2 * p):LANE * (2 * p + 1)], v)
        o1 = _attn_head(qa_ref[0, :, LANE * (2 * p + 1):LANE * (2 * p + 2)],
                        ka_ref[0, :, LANE * (2 * p + 1):LANE * (2 * p + 2)], v)
        o_ref[0, :, LANE * p:LANE * (p + 1)] = jnp.where(lo, o0, o1).astype(o_ref.dtype)
    base = MLA_HEADS // 2
    for p in range(GQA_HEADS // 2):
        k = kb_ref[0, :, LANE * p:LANE * (p + 1)]
        v = vb_ref[0, :, LANE * p:LANE * (p + 1)]
        o0 = _attn_head(qb_ref[0, :, LANE * (2 * p):LANE * (2 * p + 1)], k, v)
        o1 = _attn_head(qb_ref[0, :, LANE * (2 * p + 1):LANE * (2 * p + 2)], k, v)
        o_ref[0, :, LANE * (base + p):LANE * (base + p + 1)] = jnp.where(lo, o0, o1).astype(o_ref.dtype)


def _attention(qa, ka, va, qb, kb, vb, q_start, q_len, k_start, k_len):
    B = qa.shape[0]
    tq = TOK_TILE
    nq = q_len // tq
    q0 = q_start // tq
    kblk = k_start // k_len
    qs = lambda w: pl.BlockSpec((1, tq, w), lambda b, i: (b, q0 + i, 0))
    ks = lambda w: pl.BlockSpec((1, k_len, w), lambda b, i: (b, kblk, 0))
    wo = (MLA_HEADS * MLA_V + GQA_HEADS * GQA_DIM)
    return pl.pallas_call(
        _attn_kernel,
        grid=(B, nq),
        in_specs=[qs(qa.shape[2]), ks(ka.shape[2]), ks(va.shape[2]),
                  qs(qb.shape[2]), ks(kb.shape[2]), ks(vb.shape[2])],
        out_specs=pl.BlockSpec((1, tq, wo), lambda b, i: (b, i, 0)),
        out_shape=jax.ShapeDtypeStruct((B, q_len, wo), BF16),
        compiler_params=_params("parallel", "arbitrary"),
        name="attention",
    )(qa, ka, va, qb, kb, vb)


def _gdn_kernel(n_lat, n_all, qkv_ref, beta_ref, g_ref, o_ref, mp_ref, qp_ref, cc_ref, gl_ref):
    d = pl.program_id(1)
    C, W = GDN_CHUNK, GDN_WIDTH
    sign = 1 - 2 * d

    ri = lax.broadcasted_iota(jnp.int32, (C, W), 0)
    cj = lax.broadcasted_iota(jnp.int32, (C, W), 1) % C
    incl = (ri - cj) * sign >= 0
    diag = ri == cj
    eye_lb = jnp.where(diag, 1.0, 0.0).astype(F32)
    r2 = lax.broadcasted_iota(jnp.int32, (C, C), 0)
    c2 = lax.broadcasted_iota(jnp.int32, (C, C), 1)
    tri = jnp.where((r2 - c2) * sign >= 0, 1.0, 0.0).astype(F32)
    ones8 = jnp.ones((SUBLANE, C), F32)
    rb = lax.broadcasted_iota(jnp.int32, (GDN_HEADS * C, W), 0) // C
    cbk = lax.broadcasted_iota(jnp.int32, (GDN_HEADS * C, W), 1) // C
    bmask = jnp.where(rb == cbk, 1.0, 0.0).astype(BF16)
    bmask_f = bmask.astype(F32)

    def bd(y):
        yb = y.astype(BF16)
        return jnp.concatenate([yb] * GDN_HEADS, axis=0) * bmask

    def mm(x, ybd):
        return _dot(x.astype(BF16), ybd)

    def diag_blocks(full):
        acc = full[0:C]
        for h in range(1, GDN_HEADS):
            acc = acc + full[C * h:C * (h + 1)]
        return acc

    @pl.when(d == 0)
    def _():
        o_ref[...] = jnp.zeros(o_ref.shape, o_ref.dtype)

    def phase1(n, carry):
        r0 = pl.multiple_of(n * C, C)
        q = qkv_ref[0, pl.ds(r0, C), 0:W]
        k = qkv_ref[0, pl.ds(r0, C), W:2 * W]
        v = qkv_ref[0, pl.ds(r0, C), 2 * W:3 * W]
        beta = beta_ref[0, pl.ds(r0, C), :]
        g = g_ref[0, pl.ds(r0, C), :]
        gc = jnp.dot(tri, g, precision=HIGHEST, preferred_element_type=F32)
        g2 = jnp.dot(ones8, gc * eye_lb, precision=HIGHEST, preferred_element_type=F32)
        dlt = gc - jnp.broadcast_to(g2[0:1], (C, W))
        dec_incl = jnp.where(incl, jnp.exp(jnp.where(incl, dlt, 0.0)), 0.0)
        dec_strict = jnp.where(diag, 0.0, dec_incl)
        gcl = jnp.where(d == 0, gc[C - 1:C], gc[0:1])
        egc = jnp.exp(gc)
        gl = jnp.exp(gcl)
        ekd = jnp.exp(gcl - gc)

        kbf = k.astype(BF16)
        kbd_t = jnp.concatenate([kbf] * GDN_HEADS, axis=0) * bmask
        gram = _dot_nt(jnp.concatenate([kbf, q.astype(BF16)], axis=0), kbd_t)
        a_mat = gram[0:C] * beta * dec_strict
        a_intra = gram[C:2 * C] * dec_incl

        b_k = -a_mat
        p_k = eye_lb + b_k
        b_k = mm(b_k, bd(b_k))
        for _ in range(4):
            r = mm(jnp.concatenate([b_k, p_k], axis=0), bd(b_k))
            b_k = r[0:C]
            p_k = p_k + r[C:2 * C]
        t_mat = p_k + mm(p_k, bd(b_k))

        kb = k * beta
        u = mm(t_mat, bd(v * beta))
        w = mm(t_mat, bd(kb * egc))
        wu = jnp.concatenate([w, u], axis=1).astype(BF16)
        kd_t = jnp.transpose(k * ekd).astype(BF16)
        kwu = _dot(kd_t, wu)
        kw = diag_blocks(kwu[:, 0:W] * bmask_f)
        ku = diag_blocks(kwu[:, W:2 * W] * bmask_f)
        a_b = a_intra.astype(BF16)
        aw = _dot(a_b, bd(w))
        au = _dot(a_b, bd(u))
        mp_ref[n] = (-kw).astype(BF16)
        cc_ref[n] = ku
        qp_ref[n] = (q * egc - aw).astype(BF16)
        gl_ref[n] = jnp.broadcast_to(gl, (SUBLANE, W))
        o_ref[0, pl.ds(r0, C), :] += au
        return carry

    lax.fori_loop(0, n_all, phase1, 0)

    n_ctx = n_all - n_lat

    def phase2(i, s):
        fwd = jnp.where(i < n_ctx, n_lat + i, i - n_ctx)
        n = jnp.where(d == 0, fwd, n_all - 1 - i)
        r0 = pl.multiple_of(n * C, C)
        r = _dot(jnp.concatenate([mp_ref[n], qp_ref[n]], axis=0), bd(s))
        o_ref[0, pl.ds(r0, C), :] += r[C:2 * C]
        return gl_ref[n][0:1] * s + r[0:C] + cc_ref[n]

    lax.fori_loop(0, n_all, phase2, jnp.zeros((C, W), F32))


def _gdn(qkv, bgx, n_lat_tokens):
    B, LT, _ = qkv.shape
    n_all = LT // GDN_CHUNK
    n_lat = n_lat_tokens // GDN_CHUNK
    W = GDN_WIDTH
    return pl.pallas_call(
        functools.partial(_gdn_kernel, n_lat, n_all),
        grid=(B, 2),
        in_specs=[
            pl.BlockSpec((1, LT, 3 * W), lambda b, d: (b, 0, 0)),
            pl.BlockSpec((1, LT, W), lambda b, d: (b, 0, d)),
            pl.BlockSpec((1, LT, W), lambda b, d: (b, 0, 2 + d)),
        ],
        out_specs=pl.BlockSpec((1, LT, W), lambda b, d: (b, 0, 0)),
        out_shape=jax.ShapeDtypeStruct((B, LT, W), F32),
        scratch_shapes=[
            pltpu.VMEM((n_all, GDN_CHUNK, W), BF16),
            pltpu.VMEM((n_all, GDN_CHUNK, W), BF16),
            pltpu.VMEM((n_all, GDN_CHUNK, W), F32),
            pltpu.VMEM((n_all, SUBLANE, W), F32),
        ],
        compiler_params=_params("parallel", "arbitrary"),
        name="gdn",
    )(qkv, bgx, bgx)


def _merge_kernel(final, x_ref, oa_ref, og_ref, sg_ref, gt_ref, ong_ref, gsum_ref, wout_ref, fg_ref, o_ref):
    wa = oa_ref.shape[2]
    og = og_ref[0]
    ogn = og * lax.rsqrt(_group_sum(og * og, gsum_ref[...]) * (1.0 / GDN_DV) + EPS) * ong_ref[...]
    ma = (oa_ref[0].astype(F32) * sg_ref[0, :, 0:wa].astype(F32)).astype(BF16)
    mg = (ogn * sg_ref[0, :, wa:].astype(F32)).astype(BF16)
    y = _dot(ma, wout_ref[0:wa, :]) + _dot(mg, wout_ref[wa:, :])
    xn = x_ref[0] + gt_ref[0] * y
    if final:
        xn = xn * lax.rsqrt(jnp.mean(xn * xn, axis=-1, keepdims=True) + EPS) * fg_ref[...]
    o_ref[0] = xn


def _merge(x, o_attn, o_gdn, sg, gt, lw, gsum, final_g, row_start, final):
    B, L, D = x.shape
    nt = L // TOK_TILE
    t0 = row_start // TOK_TILE
    full = lambda a: pl.BlockSpec(a.shape, lambda b, t: (0,) * a.ndim)
    return pl.pallas_call(
        functools.partial(_merge_kernel, final),
        grid=(B, nt),
        in_specs=[
            pl.BlockSpec((1, TOK_TILE, D), lambda b, t: (b, t, 0)),
            pl.BlockSpec((1, TOK_TILE, o_attn.shape[2]), lambda b, t: (b, t, 0)),
            pl.BlockSpec((1, TOK_TILE, GDN_WIDTH), lambda b, t: (b, t0 + t, 0)),
            pl.BlockSpec((1, TOK_TILE, D_MIX), lambda b, t: (b, t0 + t, 0)),
            pl.BlockSpec((1, 1, D), lambda b, t: (b, 0, 0)),
            full(lw["on_g"]), full(gsum), full(lw["w_out"]), full(final_g),
        ],
        out_specs=pl.BlockSpec((1, TOK_TILE, D), lambda b, t: (b, t, 0)),
        out_shape=jax.ShapeDtypeStruct((B, L, D), F32),
        compiler_params=_params("parallel", "arbitrary"),
        name="merge",
    )(x, o_attn, o_gdn, sg, gt, lw["on_g"], gsum, lw["w_out"], final_g)


def _layer_weights(l, w_in, norm_g, mla_qn_g, mla_w_uq, mla_kvn_g, mla_w_ukv, gqa_qn_g, gqa_kn_g,
                   gdn_conv_w, gdn_a_log, gdn_dt_bias, gdn_on_g, w_out):
    D = D_MODEL
    w = w_in[l]
    splits = np.cumsum([0, MLA_Q_RANK, MLA_KV_RANK, MLA_ROPE, GQA_HEADS * GQA_DIM, GQA_KV_HEADS * GQA_DIM,
                        GQA_KV_HEADS * GQA_DIM, 3 * GDN_HEADS * GDN_DK, 2 * GDN_HEADS, 2 * GDN_HEADS, D_MIX])
    seg = [w[:, int(a):int(b)] for a, b in zip(splits[:-1], splits[1:])]
    cq, ckv, kr, gq, gk, gv, gdn, bb, aa, gate = seg
    z = lambda n: jnp.zeros((D, n), w.dtype)
    kr_p = jnp.concatenate([z(MLA_NOPE), kr, z(LANE - MLA_NOPE - MLA_ROPE)], axis=1)
    k0, k1 = gk[:, :GQA_DIM], gk[:, GQA_DIM:]
    v0, v1 = gv[:, :GQA_DIM], gv[:, GQA_DIM:]
    gk3 = jnp.concatenate([k0, k0, k0, k1, k1, k1], axis=1)
    gv3 = jnp.concatenate([v0, v0, v0, v1, v1, v1], axis=1)
    bx = jnp.repeat(bb, GDN_DV, axis=1)
    ax = jnp.repeat(aa, GDN_DV, axis=1)
    w_p = jnp.concatenate([cq, ckv, kr_p, gq, gk3, gv3, gdn, bx, ax, gate], axis=1).astype(BF16)

    dqk = MLA_NOPE + MLA_ROPE
    uq = mla_w_uq[l].reshape(MLA_Q_RANK, MLA_HEADS, dqk)
    uq_p = jnp.concatenate([uq, jnp.zeros((MLA_Q_RANK, MLA_HEADS, LANE - dqk), uq.dtype)], axis=2)
    ukv = mla_w_ukv[l].reshape(MLA_KV_RANK, MLA_HEADS, MLA_NOPE + MLA_V)
    uk_p = jnp.concatenate([ukv[:, :, :MLA_NOPE],
                            jnp.zeros((MLA_KV_RANK, MLA_HEADS, LANE - MLA_NOPE), ukv.dtype)], axis=2)
    uv = ukv[:, :, MLA_NOPE:]
    return {
        "norm_g": norm_g[l].reshape(1, D),
        "w_in": w_p,
        "qn_g": mla_qn_g[l].reshape(1, -1),
        "w_uq": uq_p.reshape(MLA_Q_RANK, MLA_HEADS * LANE).astype(BF16),
        "kvn_g": mla_kvn_g[l].reshape(1, -1),
        "w_uk": uk_p.reshape(MLA_KV_RANK, MLA_HEADS * LANE).astype(BF16),
        "w_uv": uv.reshape(MLA_KV_RANK, MLA_HEADS * MLA_V).astype(BF16),
        "gq_g": (jnp.tile(gqa_qn_g[l], GQA_HEADS) * GQA_SCALE).reshape(1, -1),
        "gk_g": jnp.tile(gqa_kn_g[l], 2 * 3).reshape(1, -1),
        "conv_w": jnp.concatenate([gdn_conv_w[l], jnp.zeros((SUBLANE - GDN_CONV, _W_GDN), F32)], axis=0),
        "alog_x": jnp.repeat(gdn_a_log[l].reshape(-1), GDN_DV).reshape(1, -1),
        "dtb_x": jnp.repeat(gdn_dt_bias[l].reshape(-1), GDN_DV).reshape(1, -1),
        "on_g": jnp.tile(gdn_on_g[l], GDN_HEADS).reshape(1, -1),
        "w_out": w_out[l].astype(BF16),
    }


def _rope_tables(seq_len):
    LT = seq_len + CTX_LEN
    t = np.arange(seq_len)
    rows, cols = (t // GRID_W).astype(np.float64), (t % GRID_W).astype(np.float64)

    def fill(c, s, lane0, d, pos):
        half = d // 2
        inv = ROPE_BASE ** (-np.arange(0, d, 2, dtype=np.float64) / d)
        ang = pos[:, None] * inv[None, :]
        c[:seq_len, lane0:lane0 + half] = np.cos(ang)
        c[:seq_len, lane0 + half:lane0 + d] = np.cos(ang)
        s[:seq_len, lane0:lane0 + half] = -np.sin(ang)
        s[:seq_len, lane0 + half:lane0 + d] = np.sin(ang)

    ca, sa = np.ones((LT, LANE)), np.zeros((LT, LANE))
    fill(ca, sa, MLA_NOPE, MLA_ROPE // 2, rows)
    fill(ca, sa, MLA_NOPE + MLA_ROPE // 2, MLA_ROPE // 2, cols)
    cb, sb = np.ones((LT, LANE)), np.zeros((LT, LANE))
    for h0 in (0, GQA_DIM):
        fill(cb, sb, h0, GQA_DIM // 2, rows)
        fill(cb, sb, h0 + GQA_DIM // 2, GQA_DIM // 2, cols)
    lane = np.arange(3 * LANE)
    gsum = (lane[:, None] // GQA_DIM == lane[None, :] // GQA_DIM).astype(np.float32)
    return {"ca": jnp.asarray(ca, F32), "sa": jnp.asarray(sa, F32),
            "cb": jnp.asarray(cb, F32), "sb": jnp.asarray(sb, F32),
            "gsum": jnp.asarray(gsum, BF16)}


def kernel(x, c, ctx, c_ctx, w_ada, b_ada, norm_g, w_in, mla_qn_g, mla_w_uq, mla_kvn_g, mla_w_ukv,
           gqa_qn_g, gqa_kn_g, gdn_conv_w, gdn_a_log, gdn_dt_bias, gdn_on_g, w_out, final_g):
    B, L, D = x.shape
    assert D == D_MODEL and L % TOK_TILE == 0 and ctx.shape[1] == CTX_LEN
    tabs = _rope_tables(L)
    rows = -(-(B + 1) // SUBLANE) * SUBLANE
    cvec = jnp.concatenate([c, c_ctx[None, :], jnp.zeros((rows - B - 1, D), F32)], axis=0)
    mod = _adaln(cvec, w_ada, b_ada)
    fg = final_g.reshape(1, D)
    gs256 = tabs["gsum"][:GDN_WIDTH, :GDN_WIDTH]

    for l in range(DEPTH):
        lw = _layer_weights(l, w_in, norm_g, mla_qn_g, mla_w_uq, mla_kvn_g, mla_w_ukv, gqa_qn_g,
                            gqa_kn_g, gdn_conv_w, gdn_a_log, gdn_dt_bias, gdn_on_g, w_out)
        ml = mod[l, :B].reshape(B, 1, 3 * D)
        mc = mod[l, B].reshape(1, 3 * D)
        shl, scl, gtl = ml[..., :D], ml[..., D:2 * D], ml[..., 2 * D:]
        shc, scc, gtc = mc[:, :D], mc[:, D:2 * D], mc[:, 2 * D:]
        qa, ka, va, qb, kb, vb, qkv, bgx, sg = _inproj(x, ctx, shl, scl, shc, scc, lw, tabs)
        o_gdn = _gdn(qkv, bgx, L)
        o_lat = _attention(qa, ka, va, qb, kb, vb, 0, L, 0, L + CTX_LEN)
        last = l == DEPTH - 1
        x_new = _merge(x, o_lat, o_gdn, sg, gtl, lw, gs256, fg, 0, last)
        if not last:
            o_ctx = _attention(qa, ka, va, qb, kb, vb, L, CTX_LEN, L, CTX_LEN)
            gtc_b = jnp.broadcast_to(gtc.reshape(1, 1, D), (B, 1, D))
            ctx = _merge(ctx, o_ctx, o_gdn, sg, gtc_b, lw, gs256, fg, L, False)
        x = x_new
    return x
```

```python
import functools
import math

import numpy as np
import jax
import jax.numpy as jnp
from jax import lax
from jax.experimental import pallas as pl
from jax.experimental.pallas import tpu as pltpu

F32 = jnp.float32
BF16 = jnp.bfloat16
HIGHEST = lax.Precision.HIGHEST

D_MODEL = 1024
DEPTH = 2
CTX_LEN = 256
GRID_W = 64
D_MIX = 1024
EPS = 1e-6
ROPE_BASE = 10000.0

MLA_HEADS = 6
MLA_NOPE = 64
MLA_ROPE = 32
MLA_V = 64
MLA_Q_RANK = 384
MLA_KV_RANK = 256
LOG2E = math.log2(math.e)
MLA_SCALE = (MLA_NOPE + MLA_ROPE) ** -0.5

GQA_HEADS = 6
GQA_KV_HEADS = 2
GQA_DIM = 64
GQA_SCALE = GQA_DIM ** -0.5

GDN_HEADS = 4
GDN_DK = 64
GDN_DV = 64
GDN_WIDTH = GDN_HEADS * GDN_DV
GDN_CONV = 5
GDN_CHUNK = 64

LANE = 128
SUBLANE = 8
TOK_TILE = 256
ATTN_KEY_CHUNK = 768
GDN_UNROLL = 9
VMEM_LIMIT = 48 * 1024 * 1024

_W_CQ = MLA_Q_RANK
_W_CKV = MLA_KV_RANK
_W_KR = LANE
_W_GQ = GQA_HEADS * GQA_DIM
_W_GK2 = 2 * LANE
_W_GV = GQA_KV_HEADS * GQA_DIM
_W_GDN = 3 * GDN_WIDTH
_W_BX = 2 * GDN_WIDTH
_W_AX = 2 * GDN_WIDTH
_W_GATE = D_MIX
_OFFS = np.cumsum([0, _W_CQ, _W_CKV, _W_KR, _W_GQ, _W_GK2, _W_GDN, _W_BX, _W_AX, _W_GATE])
(_O_CQ, _O_CKV, _O_KR, _O_GQ, _O_GK2, _O_GDN, _O_BX, _O_AX, _O_GATE, D_IN_P) = [int(v) for v in _OFFS]


def _dot(a, b):
    return jnp.dot(a, b, preferred_element_type=F32)


def _dot_nt(a, b):
    return lax.dot_general(a, b, (((1,), (1,)), ((), ())), preferred_element_type=F32)


def _params(*sem):
    return pltpu.CompilerParams(dimension_semantics=sem, vmem_limit_bytes=VMEM_LIMIT)


def _adaln_kernel(c_ref, w_ref, b_ref, o_ref):
    c = c_ref[...]
    s = c * jax.nn.sigmoid(c)
    o_ref[0] = jnp.dot(s, w_ref[0], precision=HIGHEST, preferred_element_type=F32) + b_ref[0]


def _adaln(cvec, w_ada, b_ada):
    rows = cvec.shape[0]
    nblk = w_ada.shape[2] // D_MODEL
    return pl.pallas_call(
        _adaln_kernel,
        grid=(DEPTH, nblk),
        in_specs=[
            pl.BlockSpec((rows, D_MODEL), lambda l, j: (0, 0)),
            pl.BlockSpec((1, D_MODEL, D_MODEL), lambda l, j: (l, 0, j)),
            pl.BlockSpec((1, 1, D_MODEL), lambda l, j: (l, 0, j)),
        ],
        out_specs=pl.BlockSpec((1, rows, D_MODEL), lambda l, j: (l, 0, j)),
        out_shape=jax.ShapeDtypeStruct((DEPTH, rows, 3 * D_MODEL), F32),
        compiler_params=_params("arbitrary", "arbitrary"),
        name="adaln",
    )(cvec, w_ada, b_ada.reshape(DEPTH, 1, 3 * D_MODEL))


def _group_sum(sq, gsum):
    hi = sq.astype(BF16)
    lo = (sq - hi.astype(F32)).astype(BF16)
    return _dot(hi, gsum) + _dot(lo, gsum)


def _rope_block(xb, c, s, half):
    lane = lax.broadcasted_iota(jnp.int32, xb.shape, 1)
    up = pltpu.roll(xb, LANE - half, 1)
    dn = pltpu.roll(xb, half, 1)
    partner = jnp.where((lane % (2 * half)) < half, up, dn)
    return xb * c + partner * s


def _inproj_kernel(nt_lat, x_ref, xp_ref, xn_ref, ctx_ref, shl_ref, scl_ref, shc_ref, scc_ref,
                   ng_ref, win_ref, qng_ref, wuq_ref, kvng_ref, wuk_ref, wuvt_ref, wgvt_ref,
                   gqg_ref, gkg_ref, cw_ref, alog_ref, dtb_ref, ca_ref, sa_ref, cb_ref, sb_ref,
                   gsum_ref,
                   qa_ref, ka_ref, va_ref, qb_ref, kb_ref, vb_ref, qkv_ref, bgx_ref, sg_ref):
    t = pl.program_id(1)
    is_ctx = t == nt_lat
    tm = TOK_TILE
    xt = jnp.where(is_ctx, ctx_ref[0], x_ref[0])
    xe = jnp.concatenate([xp_ref[0], xt, xn_ref[0]], axis=0)
    sh = jnp.where(is_ctx, shc_ref[...], shl_ref[0])
    sc = jnp.where(is_ctx, scc_ref[...], scl_ref[0])
    ms = jnp.mean(xe * xe, axis=-1, keepdims=True)
    he = (xe * lax.rsqrt(ms + EPS) * ng_ref[...]) * (1.0 + sc) + sh
    hbe = he.astype(BF16)
    hb = he[SUBLANE:SUBLANE + tm].astype(BF16)

    ca, sa = ca_ref[...], sa_ref[...]
    cb, sb = cb_ref[...], sb_ref[...]

    zcq = _dot(hb, win_ref[:, _O_CQ:_O_CQ + _W_CQ])
    cqn = zcq * lax.rsqrt(jnp.mean(zcq * zcq, axis=-1, keepdims=True) + EPS) * qng_ref[...]
    q = _dot(cqn.astype(BF16), wuq_ref[...]) * (MLA_SCALE * LOG2E)
    for h in range(MLA_HEADS):
        sl = slice(LANE * h, LANE * (h + 1))
        qa_ref[0, :, sl] = _rope_block(q[:, sl], ca, sa, MLA_ROPE // 4).astype(BF16)

    zckv = _dot(hb, win_ref[:, _O_CKV:_O_CKV + _W_CKV])
    ckvn = (zckv * lax.rsqrt(jnp.mean(zckv * zckv, axis=-1, keepdims=True) + EPS) * kvng_ref[...]).astype(BF16)
    kr = _rope_block(_dot(hb, win_ref[:, _O_KR:_O_KR + _W_KR]), ca, sa, MLA_ROPE // 4)
    kn = _dot(ckvn, wuk_ref[...])
    for h in range(MLA_HEADS):
        sl = slice(LANE * h, LANE * (h + 1))
        ka_ref[0, :, sl] = (kn[:, sl] + kr).astype(BF16)
    va_ref[0] = _dot_nt(wuvt_ref[...], ckvn).astype(BF16)

    gsum = gsum_ref[...]
    zq = _dot(hb, win_ref[:, _O_GQ:_O_GQ + _W_GQ])
    qn = zq * lax.rsqrt(_group_sum(zq * zq, gsum) * (1.0 / GQA_DIM) + EPS) * gqg_ref[...]
    lane = lax.broadcasted_iota(jnp.int32, (tm, LANE), 1)
    for p in range(GQA_HEADS // 2):
        blk = _rope_block(qn[:, LANE * p:LANE * (p + 1)], cb, sb, GQA_DIM // 4)
        qb_ref[0, :, LANE * (2 * p):LANE * (2 * p + 1)] = jnp.where(lane < GQA_DIM, blk, 0.0).astype(BF16)
        qb_ref[0, :, LANE * (2 * p + 1):LANE * (2 * p + 2)] = jnp.where(lane >= GQA_DIM, blk, 0.0).astype(BF16)
    zk = _dot(hb, win_ref[:, _O_GK2:_O_GK2 + _W_GK2])
    kn2 = zk * lax.rsqrt(_group_sum(zk * zk, gsum[:_W_GK2, :_W_GK2]) * (1.0 / GQA_DIM) + EPS) * gkg_ref[...]
    for p in range(2):
        sl = slice(LANE * p, LANE * (p + 1))
        kb_ref[0, :, sl] = _rope_block(kn2[:, sl], cb, sb, GQA_DIM // 4).astype(BF16)
    vb_ref[0] = _dot_nt(wgvt_ref[...], hb).astype(BF16)

    zg = _dot(hbe, win_ref[:, _O_GDN:_O_GDN + _W_GDN])
    row = lax.broadcasted_iota(jnp.int32, (tm + 2 * SUBLANE, 1), 0)
    has_prev = jnp.logical_and(t > 0, jnp.logical_not(is_ctx))
    has_next = t < nt_lat - 1
    keep = jnp.logical_and(jnp.logical_or(row >= SUBLANE, has_prev),
                           jnp.logical_or(row < SUBLANE + tm, has_next))
    zg = jnp.where(keep, zg, 0.0)
    pad = GDN_CONV // 2
    y = jnp.zeros((tm, _W_GDN), F32)
    for j in range(GDN_CONV):
        o = SUBLANE - pad + j
        y = y + zg[o:o + tm] * cw_ref[j:j + 1, :]
    y = y * jax.nn.sigmoid(y)
    gs = gsum[:GDN_WIDTH, :GDN_WIDTH]
    qg = y[:, :GDN_WIDTH]
    kg = y[:, GDN_WIDTH:2 * GDN_WIDTH]
    qkv_ref[0, :, 0:GDN_WIDTH] = qg * lax.rsqrt(_group_sum(qg * qg, gs) + EPS) * (GDN_DK ** -0.5)
    qkv_ref[0, :, GDN_WIDTH:2 * GDN_WIDTH] = kg * lax.rsqrt(_group_sum(kg * kg, gs) + EPS)
    qkv_ref[0, :, 2 * GDN_WIDTH:] = y[:, 2 * GDN_WIDTH:]
    zb = _dot(hb, win_ref[:, _O_BX:_O_BX + _W_BX])
    bgx_ref[0, :, 0:_W_BX] = jax.nn.sigmoid(zb)
    za = _dot(hb, win_ref[:, _O_AX:_O_AX + _W_AX]) + dtb_ref[...]
    softplus = jnp.maximum(za, 0.0) + jnp.log(1.0 + jnp.exp(-jnp.abs(za)))
    bgx_ref[0, :, _W_BX:] = -jnp.exp(alog_ref[...]) * softplus

    zgate = _dot(hb, win_ref[:, _O_GATE:_O_GATE + _W_GATE])
    sg_ref[0] = (zgate * jax.nn.sigmoid(zgate)).astype(BF16)


def _inproj(x, ctx, shl, scl, shc, scc, lw, tabs):
    B, L, D = x.shape
    nt_lat = L // TOK_TILE
    nt = nt_lat + CTX_LEN // TOK_TILE
    LT = L + CTX_LEN
    rb = TOK_TILE // SUBLANE
    nrb = L // SUBLANE
    full = lambda a: pl.BlockSpec(a.shape, lambda b, t: (0,) * a.ndim)
    tok = lambda w: pl.BlockSpec((1, TOK_TILE, w), lambda b, t: (b, t, 0))
    tab = pl.BlockSpec((TOK_TILE, LANE), lambda b, t: (t, 0))
    mod = pl.BlockSpec((1, 1, D), lambda b, t: (b, 0, 0))
    consts = [lw["norm_g"], lw["w_in"], lw["qn_g"], lw["w_uq"], lw["kvn_g"], lw["w_uk"], lw["w_uv_t"], lw["w_gv_t"],
              lw["gq_g"], lw["gk_g"], lw["conv_w"], lw["alog_x"], lw["dtb_x"]]
    in_specs = [
        pl.BlockSpec((1, TOK_TILE, D), lambda b, t: (b, jnp.minimum(t, nt_lat - 1), 0)),
        pl.BlockSpec((1, SUBLANE, D), lambda b, t: (b, jnp.clip(t * rb - 1, 0, nrb - 1), 0)),
        pl.BlockSpec((1, SUBLANE, D), lambda b, t: (b, jnp.clip((t + 1) * rb, 0, nrb - 1), 0)),
        pl.BlockSpec((1, CTX_LEN, D), lambda b, t: (b, 0, 0)),
        mod, mod, full(shc), full(scc),
    ] + [full(a) for a in consts] + [tab, tab, tab, tab, full(tabs["gsum"])]
    out_w = [(MLA_HEADS * LANE, BF16), (MLA_HEADS * LANE, BF16), None,
             (GQA_HEADS * LANE, BF16), (_W_GK2, BF16), None,
             (_W_GDN, F32), (_W_BX + _W_AX, F32), (D_MIX, BF16)]
    vt_w = {2: MLA_HEADS * MLA_V, 5: _W_GV}
    out_specs = [pl.BlockSpec((1, vt_w[i], TOK_TILE), lambda b, t: (b, 0, t)) if i in vt_w else tok(o[0])
                 for i, o in enumerate(out_w)]
    out_shape = [jax.ShapeDtypeStruct((B, vt_w[i], LT), BF16) if i in vt_w
                 else jax.ShapeDtypeStruct((B, LT, o[0]), o[1]) for i, o in enumerate(out_w)]
    return pl.pallas_call(
        functools.partial(_inproj_kernel, nt_lat),
        grid=(B, nt),
        in_specs=in_specs,
        out_specs=out_specs,
        out_shape=out_shape,
        compiler_params=_params("parallel", "arbitrary"),
        name="inproj",
    )(x, x, x, ctx, shl, scl, shc, scc, *consts,
      tabs["ca"], tabs["sa"], tabs["cb"], tabs["sb"], tabs["gsum"])


def _attn_kernel(qa_ref, ka_ref, va_ref, qb_ref, kb_ref, vb_ref, o_ref):
    ones_rows = jnp.ones((2 * SUBLANE, ka_ref.shape[1]), BF16)
    group = GQA_HEADS // GQA_KV_HEADS
    heads = []
    for h in range(MLA_HEADS):
        sl = slice(LANE * h, LANE * (h + 1))
        heads.append((qa_ref.at[0, :, sl], ka_ref.at[0, :, sl], va_ref.at[0, MLA_V * h:MLA_V * (h + 1), :]))
    for j in range(GQA_HEADS):
        kv = j // group
        kblk = 0 if (j % 2) == kv else 1
        heads.append((qb_ref.at[0, :, LANE * j:LANE * (j + 1)], kb_ref.at[0, :, LANE * kblk:LANE * (kblk + 1)],
                      vb_ref.at[0, GQA_DIM * kv:GQA_DIM * (kv + 1), :]))

    lk = ka_ref.shape[1]
    kc = ATTN_KEY_CHUNK if lk % ATTN_KEY_CHUNK == 0 else lk
    chunks = [slice(c * kc, (c + 1) * kc) for c in range(lk // kc)]

    def scores(i, ck):
        return _dot_nt(heads[i][1][ck, :], heads[i][0][...])

    s_next = [scores(0, ck) for ck in chunks]
    for i in range(len(heads)):
        s_t = s_next
        m = functools.reduce(jnp.maximum, [jnp.max(s, axis=0, keepdims=True) for s in s_t])
        dv = heads[i][2].shape[0]
        r = None
        s_next = []
        for c, ck in enumerate(chunks):
            if i + 1 < len(heads):
                s_next.append(scores(i + 1, ck))
            p_t = jnp.exp2(s_t[c] - m).astype(BF16)
            lhs = jnp.concatenate([heads[i][2][:, ck], ones_rows[:, ck]], axis=0)
            rc = _dot(lhs, p_t)
            r = rc if r is None else r + rc
        o_ref[0, dv * i:dv * (i + 1), :] = (r[0:dv] / r[dv:dv + 1]).astype(o_ref.dtype)


def _attention(qa, ka, va_t, qb, kb, vb_t, q_start, q_len, k_start, k_len):
    B = qa.shape[0]
    tq = TOK_TILE
    nq = q_len // tq
    q0 = q_start // tq
    kblk = k_start // k_len
    qs = lambda w: pl.BlockSpec((1, tq, w), lambda b, i: (b, q0 + i, 0))
    ks = lambda w: pl.BlockSpec((1, k_len, w), lambda b, i: (b, kblk, 0))
    vs = lambda w: pl.BlockSpec((1, w, k_len), lambda b, i: (b, 0, kblk))
    wo = (MLA_HEADS * MLA_V + GQA_HEADS * GQA_DIM)
    return pl.pallas_call(
        _attn_kernel,
        grid=(B, nq),
        in_specs=[qs(qa.shape[2]), ks(ka.shape[2]), vs(va_t.shape[1]),
                  qs(qb.shape[2]), ks(kb.shape[2]), vs(vb_t.shape[1])],
        out_specs=pl.BlockSpec((1, wo, tq), lambda b, i: (b, 0, i)),
        out_shape=jax.ShapeDtypeStruct((B, wo, q_len), BF16),
        compiler_params=_params("parallel", "arbitrary"),
        name="attention",
    )(qa, ka, va_t, qb, kb, vb_t)


def _gdn_kernel(n_lat, n_all, qkv_ref, beta_ref, g_ref, o_ref, mp_ref, qp_ref, cc_ref, gl_ref):
    d = pl.program_id(1)
    C, W = GDN_CHUNK, GDN_WIDTH
    sign = 1 - 2 * d

    ri = lax.broadcasted_iota(jnp.int32, (C, W), 0)
    cj = lax.broadcasted_iota(jnp.int32, (C, W), 1) % C
    incl = (ri - cj) * sign >= 0
    incl_t = jnp.where((cj - ri) * sign >= 0, 1.0, 0.0).astype(F32)
    diag = ri == cj
    eye_lb = jnp.where(diag, 1.0, 0.0).astype(F32)
    r2 = lax.broadcasted_iota(jnp.int32, (C, C), 0)
    c2 = lax.broadcasted_iota(jnp.int32, (C, C), 1)
    tri = jnp.where((r2 - c2) * sign >= 0, 1.0, 0.0).astype(F32)
    ones8 = jnp.ones((SUBLANE, C), F32)
    rb = lax.broadcasted_iota(jnp.int32, (GDN_HEADS * C, W), 0) // C
    cbk = lax.broadcasted_iota(jnp.int32, (GDN_HEADS * C, W), 1) // C
    bmask = jnp.where(rb == cbk, 1.0, 0.0).astype(BF16)
    bmask_f = bmask.astype(F32)

    def bd(y):
        yb = y.astype(BF16)
        return jnp.concatenate([yb] * GDN_HEADS, axis=0) * bmask

    def mm(x, ybd):
        return _dot(x.astype(BF16), ybd)

    def diag_blocks(full):
        acc = full[0:C]
        for h in range(1, GDN_HEADS):
            acc = acc + full[C * h:C * (h + 1)]
        return acc

    @pl.when(d == 0)
    def _():
        o_ref[...] = jnp.zeros(o_ref.shape, o_ref.dtype)

    def phase1(it, carry):
        U = GDN_UNROLL
        each = lambda f, *ls: [f(*a) for a in zip(*ls)]
        r0s = [pl.multiple_of((it * U + c) * C, C) for c in range(U)]
        q = [qkv_ref[0, pl.ds(r0, C), 0:W] for r0 in r0s]
        k = [qkv_ref[0, pl.ds(r0, C), W:2 * W] for r0 in r0s]
        v = [qkv_ref[0, pl.ds(r0, C), 2 * W:3 * W] for r0 in r0s]
        beta = [beta_ref[0, pl.ds(r0, C), :] for r0 in r0s]
        g = [g_ref[0, pl.ds(r0, C), :] for r0 in r0s]
        gc = each(lambda x: jnp.dot(tri, x, precision=HIGHEST, preferred_element_type=F32), g)
        g2 = each(lambda x: jnp.dot(ones8, x * incl_t, precision=HIGHEST, preferred_element_type=F32), g)
        kbf = each(lambda x: x.astype(BF16), k)
        gram = each(lambda kb_, q_: _dot_nt(jnp.concatenate([kb_, q_.astype(BF16)], axis=0),
                                            jnp.concatenate([kb_] * GDN_HEADS, axis=0) * bmask), kbf, q)
        dlt = each(lambda a, b: a - jnp.broadcast_to(b[0:1], (C, W)), gc, g2)
        dec_incl = each(lambda x: jnp.where(incl, jnp.exp(jnp.where(incl, x, 0.0)), 0.0), dlt)
        gcl = each(lambda x: jnp.where(d == 0, x[C - 1:C], x[0:1]), gc)
        egc = each(jnp.exp, gc)
        a_mat = each(lambda gr, b, de: gr[0:C] * b * jnp.where(diag, 0.0, de), gram, beta, dec_incl)
        a_intra = each(lambda gr, de: (gr[C:2 * C] * de).astype(BF16), gram, dec_incl)

        b_k = each(lambda a: -a, a_mat)
        p_k = each(lambda b: eye_lb + b, b_k)
        b_k = each(lambda b: mm(b, bd(b)), b_k)
        for _ in range(4):
            r = each(lambda b, p: mm(jnp.concatenate([b, p], axis=0), bd(b)), b_k, p_k)
            b_k = each(lambda x: x[0:C], r)
            p_k = each(lambda p, x: p + x[C:2 * C], p_k, r)
        t_mat = each(lambda p, b: p + mm(p, bd(b)), p_k, b_k)

        u = each(lambda t, v_, b: mm(t, bd(v_ * b)), t_mat, v, beta)
        w = each(lambda t, k_, b, e: mm(t, bd(k_ * b * e)), t_mat, k, beta, egc)
        kd_t = each(lambda k_, gl_, gc_: jnp.transpose(k_ * jnp.exp(gl_ - gc_)).astype(BF16), k, gcl, gc)
        kwu = each(lambda kt, w_, u_: _dot(kt, jnp.concatenate([w_, u_], axis=1).astype(BF16)), kd_t, w, u)
        aw = each(lambda a, w_: _dot(a, bd(w_)), a_intra, w)
        au = each(lambda a, u_: _dot(a, bd(u_)), a_intra, u)
        for c in range(U):
            n = it * U + c
            mp_ref[n] = (-diag_blocks(kwu[c][:, 0:W] * bmask_f)).astype(BF16)
            cc_ref[n] = diag_blocks(kwu[c][:, W:2 * W] * bmask_f)
            qp_ref[n] = (q[c] * egc[c] - aw[c]).astype(BF16)
            gl_ref[n] = jnp.broadcast_to(jnp.exp(gcl[c]), (SUBLANE, W))
            o_ref[0, pl.ds(r0s[c], C), :] += au[c]
        return carry

    assert n_all % GDN_UNROLL == 0
    lax.fori_loop(0, n_all // GDN_UNROLL, phase1, 0)

    n_ctx = n_all - n_lat

    def phase2(i, s):
        fwd = jnp.where(i < n_ctx, n_lat + i, i - n_ctx)
        n = jnp.where(d == 0, fwd, n_all - 1 - i)
        r0 = pl.multiple_of(n * C, C)
        r = _dot(jnp.concatenate([mp_ref[n], qp_ref[n]], axis=0), bd(s))
        o_ref[0, pl.ds(r0, C), :] += r[C:2 * C]
        return gl_ref[n][0:1] * s + r[0:C] + cc_ref[n]

    lax.fori_loop(0, n_all, phase2, jnp.zeros((C, W), F32))


def _gdn(qkv, bgx, n_lat_tokens):
    B, LT, _ = qkv.shape
    n_all = LT // GDN_CHUNK
    n_lat = n_lat_tokens // GDN_CHUNK
    W = GDN_WIDTH
    return pl.pallas_call(
        functools.partial(_gdn_kernel, n_lat, n_all),
        grid=(B, 2),
        in_specs=[
            pl.BlockSpec((1, LT, 3 * W), lambda b, d: (b, 0, 0)),
            pl.BlockSpec((1, LT, W), lambda b, d: (b, 0, d)),
            pl.BlockSpec((1, LT, W), lambda b, d: (b, 0, 2 + d)),
        ],
        out_specs=pl.BlockSpec((1, LT, W), lambda b, d: (b, 0, 0)),
        out_shape=jax.ShapeDtypeStruct((B, LT, W), F32),
        scratch_shapes=[
            pltpu.VMEM((n_all, GDN_CHUNK, W), BF16),
            pltpu.VMEM((n_all, GDN_CHUNK, W), BF16),
            pltpu.VMEM((n_all, GDN_CHUNK, W), F32),
            pltpu.VMEM((n_all, SUBLANE, W), F32),
        ],
        compiler_params=_params("parallel", "arbitrary"),
        name="gdn",
    )(qkv, bgx, bgx)


def _merge_kernel(final, x_ref, oa_ref, og_ref, sg_ref, gt_ref, ong_ref, gsum_ref, wout_ref, fg_ref, o_ref):
    wa = oa_ref.shape[1]
    og = og_ref[0]
    ogn = og * lax.rsqrt(_group_sum(og * og, gsum_ref[...]) * (1.0 / GDN_DV) + EPS) * ong_ref[...]
    ma = (jnp.transpose(oa_ref[0].astype(F32)) * sg_ref[0, :, 0:wa].astype(F32)).astype(BF16)
    mg = (ogn * sg_ref[0, :, wa:].astype(F32)).astype(BF16)
    y = _dot(ma, wout_ref[0:wa, :]) + _dot(mg, wout_ref[wa:, :])
    xn = x_ref[0] + gt_ref[0] * y
    if final:
        xn = xn * lax.rsqrt(jnp.mean(xn * xn, axis=-1, keepdims=True) + EPS) * fg_ref[...]
    o_ref[0] = xn


def _merge(x, o_attn, o_gdn, sg, gt, lw, gsum, final_g, row_start, final):
    B, L, D = x.shape
    nt = L // TOK_TILE
    t0 = row_start // TOK_TILE
    full = lambda a: pl.BlockSpec(a.shape, lambda b, t: (0,) * a.ndim)
    return pl.pallas_call(
        functools.partial(_merge_kernel, final),
        grid=(B, nt),
        in_specs=[
            pl.BlockSpec((1, TOK_TILE, D), lambda b, t: (b, t, 0)),
            pl.BlockSpec((1, o_attn.shape[1], TOK_TILE), lambda b, t: (b, 0, t)),
            pl.BlockSpec((1, TOK_TILE, GDN_WIDTH), lambda b, t: (b, t0 + t, 0)),
            pl.BlockSpec((1, TOK_TILE, D_MIX), lambda b, t: (b, t0 + t, 0)),
            pl.BlockSpec((1, 1, D), lambda b, t: (b, 0, 0)),
            full(lw["on_g"]), full(gsum), full(lw["w_out"]), full(final_g),
        ],
        out_specs=pl.BlockSpec((1, TOK_TILE, D), lambda b, t: (b, t, 0)),
        out_shape=jax.ShapeDtypeStruct((B, L, D), F32),
        compiler_params=_params("parallel", "arbitrary"),
        name="merge",
    )(x, o_attn, o_gdn, sg, gt, lw["on_g"], gsum, lw["w_out"], final_g)


def _layer_weights(l, w_in, norm_g, mla_qn_g, mla_w_uq, mla_kvn_g, mla_w_ukv, gqa_qn_g, gqa_kn_g,
                   gdn_conv_w, gdn_a_log, gdn_dt_bias, gdn_on_g, w_out):
    D = D_MODEL
    w = w_in[l]
    splits = np.cumsum([0, MLA_Q_RANK, MLA_KV_RANK, MLA_ROPE, GQA_HEADS * GQA_DIM, GQA_KV_HEADS * GQA_DIM,
                        GQA_KV_HEADS * GQA_DIM, 3 * GDN_HEADS * GDN_DK, 2 * GDN_HEADS, 2 * GDN_HEADS, D_MIX])
    seg = [w[:, int(a):int(b)] for a, b in zip(splits[:-1], splits[1:])]
    cq, ckv, kr, gq, gk, gv, gdn, bb, aa, gate = seg
    z = lambda n: jnp.zeros((D, n), w.dtype)
    kr_p = jnp.concatenate([z(MLA_NOPE), kr, z(LANE - MLA_NOPE - MLA_ROPE)], axis=1)
    k0, k1 = gk[:, :GQA_DIM], gk[:, GQA_DIM:]
    gk2 = jnp.concatenate([k0, k1, k1, k0], axis=1)
    bx = jnp.repeat(bb, GDN_DV, axis=1)
    ax = jnp.repeat(aa, GDN_DV, axis=1)
    w_p = jnp.concatenate([cq, ckv, kr_p, gq, gk2, gdn, bx, ax, gate], axis=1).astype(BF16)

    dqk = MLA_NOPE + MLA_ROPE
    uq = mla_w_uq[l].reshape(MLA_Q_RANK, MLA_HEADS, dqk)
    uq_p = jnp.concatenate([uq, jnp.zeros((MLA_Q_RANK, MLA_HEADS, LANE - dqk), uq.dtype)], axis=2)
    ukv = mla_w_ukv[l].reshape(MLA_KV_RANK, MLA_HEADS, MLA_NOPE + MLA_V)
    uk_p = jnp.concatenate([ukv[:, :, :MLA_NOPE],
                            jnp.zeros((MLA_KV_RANK, MLA_HEADS, LANE - MLA_NOPE), ukv.dtype)], axis=2)
    uv = ukv[:, :, MLA_NOPE:]
    return {
        "norm_g": norm_g[l].reshape(1, D),
        "w_in": w_p,
        "qn_g": mla_qn_g[l].reshape(1, -1),
        "w_uq": uq_p.reshape(MLA_Q_RANK, MLA_HEADS * LANE).astype(BF16),
        "kvn_g": mla_kvn_g[l].reshape(1, -1),
        "w_uk": uk_p.reshape(MLA_KV_RANK, MLA_HEADS * LANE).astype(BF16),
        "w_uv_t": uv.reshape(MLA_KV_RANK, MLA_HEADS * MLA_V).T.astype(BF16),
        "w_gv_t": gv.T.astype(BF16),
        "gq_g": (jnp.tile(gqa_qn_g[l], GQA_HEADS) * (GQA_SCALE * LOG2E)).reshape(1, -1),
        "gk_g": jnp.tile(gqa_kn_g[l], 2 * 2).reshape(1, -1),
        "conv_w": jnp.concatenate([gdn_conv_w[l], jnp.zeros((SUBLANE - GDN_CONV, _W_GDN), F32)], axis=0),
        "alog_x": jnp.repeat(gdn_a_log[l].reshape(-1), GDN_DV).reshape(1, -1),
        "dtb_x": jnp.repeat(gdn_dt_bias[l].reshape(-1), GDN_DV).reshape(1, -1),
        "on_g": jnp.tile(gdn_on_g[l], GDN_HEADS).reshape(1, -1),
        "w_out": w_out[l].astype(BF16),
    }


def _rope_tables(seq_len):
    LT = seq_len + CTX_LEN
    t = np.arange(seq_len)
    rows, cols = (t // GRID_W).astype(np.float64), (t % GRID_W).astype(np.float64)

    def fill(c, s, lane0, d, pos):
        half = d // 2
        inv = ROPE_BASE ** (-np.arange(0, d, 2, dtype=np.float64) / d)
        ang = pos[:, None] * inv[None, :]
        c[:seq_len, lane0:lane0 + half] = np.cos(ang)
        c[:seq_len, lane0 + half:lane0 + d] = np.cos(ang)
        s[:seq_len, lane0:lane0 + half] = -np.sin(ang)
        s[:seq_len, lane0 + half:lane0 + d] = np.sin(ang)

    ca, sa = np.ones((LT, LANE)), np.zeros((LT, LANE))
    fill(ca, sa, MLA_NOPE, MLA_ROPE // 2, rows)
    fill(ca, sa, MLA_NOPE + MLA_ROPE // 2, MLA_ROPE // 2, cols)
    cb, sb = np.ones((LT, LANE)), np.zeros((LT, LANE))
    for h0 in (0, GQA_DIM):
        fill(cb, sb, h0, GQA_DIM // 2, rows)
        fill(cb, sb, h0 + GQA_DIM // 2, GQA_DIM // 2, cols)
    lane = np.arange(3 * LANE)
    gsum = (lane[:, None] // GQA_DIM == lane[None, :] // GQA_DIM).astype(np.float32)
    return {"ca": jnp.asarray(ca, F32), "sa": jnp.asarray(sa, F32),
            "cb": jnp.asarray(cb, F32), "sb": jnp.asarray(sb, F32),
            "gsum": jnp.asarray(gsum, BF16)}


def kernel(x, c, ctx, c_ctx, w_ada, b_ada, norm_g, w_in, mla_qn_g, mla_w_uq, mla_kvn_g, mla_w_ukv,
           gqa_qn_g, gqa_kn_g, gdn_conv_w, gdn_a_log, gdn_dt_bias, gdn_on_g, w_out, final_g):
    B, L, D = x.shape
    assert D == D_MODEL and L % TOK_TILE == 0 and ctx.shape[1] == CTX_LEN
    tabs = _rope_tables(L)
    rows = -(-(B + 1) // SUBLANE) * SUBLANE
    cvec = jnp.concatenate([c, c_ctx[None, :], jnp.zeros((rows - B - 1, D), F32)], axis=0)
    mod = _adaln(cvec, w_ada, b_ada)
    fg = final_g.reshape(1, D)
    gs256 = tabs["gsum"][:GDN_WIDTH, :GDN_WIDTH]

    for l in range(DEPTH):
        lw = _layer_weights(l, w_in, norm_g, mla_qn_g, mla_w_uq, mla_kvn_g, mla_w_ukv, gqa_qn_g,
                            gqa_kn_g, gdn_conv_w, gdn_a_log, gdn_dt_bias, gdn_on_g, w_out)
        ml = mod[l, :B].reshape(B, 1, 3 * D)
        mc = mod[l, B].reshape(1, 3 * D)
        shl, scl, gtl = ml[..., :D], ml[..., D:2 * D], ml[..., 2 * D:]
        shc, scc, gtc = mc[:, :D], mc[:, D:2 * D], mc[:, 2 * D:]
        qa, ka, va, qb, kb, vb, qkv, bgx, sg = _inproj(x, ctx, shl, scl, shc, scc, lw, tabs)
        o_gdn = _gdn(qkv, bgx, L)
        o_lat = _attention(qa, ka, va, qb, kb, vb, 0, L, 0, L + CTX_LEN)
        last = l == DEPTH - 1
        x_new = _merge(x, o_lat, o_gdn, sg, gtl, lw, gs256, fg, 0, last)
        if not last:
            o_ctx = _attention(qa, ka, va, qb, kb, vb, L, CTX_LEN, L, CTX_LEN)
            gtc_b = jnp.broadcast_to(gtc.reshape(1, 1, D), (B, 1, D))
            ctx = _merge(ctx, o_ctx, o_gdn, sg, gtc_b, lw, gs256, fg, L, False)
        x = x_new
    return x
```

```python
import functools
import math

import numpy as np
import jax
import jax.numpy as jnp
from jax import lax
from jax.experimental import pallas as pl
from jax.experimental.pallas import tpu as pltpu

F32 = jnp.float32
BF16 = jnp.bfloat16
HIGHEST = lax.Precision.HIGHEST

D_MODEL = 1024
DEPTH = 2
CTX_LEN = 256
GRID_W = 64
D_MIX = 1024
EPS = 1e-6
ROPE_BASE = 10000.0

MLA_HEADS = 6
MLA_NOPE = 64
MLA_ROPE = 32
MLA_V = 64
MLA_Q_RANK = 384
MLA_KV_RANK = 256
LOG2E = math.log2(math.e)
MLA_SCALE = (MLA_NOPE + MLA_ROPE) ** -0.5

GQA_HEADS = 6
GQA_KV_HEADS = 2
GQA_DIM = 64
GQA_SCALE = GQA_DIM ** -0.5

GDN_HEADS = 4
GDN_DK = 64
GDN_DV = 64
GDN_WIDTH = GDN_HEADS * GDN_DV
GDN_CONV = 5
GDN_CHUNK = 64

LANE = 128
SUBLANE = 8
TOK_TILE = 256
ATTN_TILES_PER_STEP = 4
GDN_UNROLL = 9
VMEM_LIMIT = 48 * 1024 * 1024

_W_CQ = MLA_Q_RANK
_W_CKV = MLA_KV_RANK
_W_KR = LANE
_W_GQ = GQA_HEADS * GQA_DIM
_W_GK2 = 2 * LANE
_W_GDN = 3 * GDN_WIDTH
_W_BX = 2 * GDN_WIDTH
_W_AX = 2 * GDN_WIDTH
_W_GATE = D_MIX
_OFFS = np.cumsum([0, _W_CQ, _W_CKV, _W_KR, _W_GQ, _W_GK2, _W_GDN, _W_BX, _W_AX, _W_GATE])
(_O_CQ, _O_CKV, _O_KR, _O_GQ, _O_GK2, _O_GDN, _O_BX, _O_AX, _O_GATE, D_IN_P) = [int(v) for v in _OFFS]


def _dot(a, b):
    return jnp.dot(a, b, preferred_element_type=F32)


def _dot_nt(a, b):
    return lax.dot_general(a, b, (((1,), (1,)), ((), ())), preferred_element_type=F32)


def _params(*sem):
    return pltpu.CompilerParams(dimension_semantics=sem, vmem_limit_bytes=VMEM_LIMIT)


def _adaln_kernel(c_ref, w_ref, b_ref, o_ref):
    c = c_ref[...]
    s = c * jax.nn.sigmoid(c)
    o_ref[0] = jnp.dot(s, w_ref[0], precision=HIGHEST, preferred_element_type=F32) + b_ref[0]


def _adaln(cvec, w_ada, b_ada):
    rows = cvec.shape[0]
    nblk = w_ada.shape[2] // D_MODEL
    return pl.pallas_call(
        _adaln_kernel,
        grid=(DEPTH, nblk),
        in_specs=[
            pl.BlockSpec((rows, D_MODEL), lambda l, j: (0, 0)),
            pl.BlockSpec((1, D_MODEL, D_MODEL), lambda l, j: (l, 0, j)),
            pl.BlockSpec((1, 1, D_MODEL), lambda l, j: (l, 0, j)),
        ],
        out_specs=pl.BlockSpec((1, rows, D_MODEL), lambda l, j: (l, 0, j)),
        out_shape=jax.ShapeDtypeStruct((DEPTH, rows, 3 * D_MODEL), F32),
        compiler_params=_params("arbitrary", "arbitrary"),
        name="adaln",
    )(cvec, w_ada, b_ada.reshape(DEPTH, 1, 3 * D_MODEL))


def _group_sum(sq, gsum):
    hi = sq.astype(BF16)
    lo = (sq - hi.astype(F32)).astype(BF16)
    return _dot(hi, gsum) + _dot(lo, gsum)


def _rope_block(xb, c, s, half):
    lane = lax.broadcasted_iota(jnp.int32, xb.shape, 1)
    up = pltpu.roll(xb, LANE - half, 1)
    dn = pltpu.roll(xb, half, 1)
    partner = jnp.where((lane % (2 * half)) < half, up, dn)
    return xb * c + partner * s


def _inproj_kernel(nt_lat, x_ref, xp_ref, xn_ref, ctx_ref, shl_ref, scl_ref, shc_ref, scc_ref,
                   ng_ref, win_ref, qng_ref, wuq_ref, kvng_ref, wuk_ref, wuvt_ref, wgvt_ref,
                   gqg_ref, gkg_ref, cw_ref, alog_ref, dtb_ref, ca_ref, sa_ref, cb_ref, sb_ref,
                   gsum_ref,
                   q_ref, k_ref, vt_ref, qkv_ref, bgx_ref, sg_ref):
    t = pl.program_id(1)
    is_ctx = t == nt_lat
    tm = TOK_TILE
    xt = jnp.where(is_ctx, ctx_ref[0], x_ref[0])
    xe = jnp.concatenate([xp_ref[0], xt, xn_ref[0]], axis=0)
    sh = jnp.where(is_ctx, shc_ref[...], shl_ref[0])
    sc = jnp.where(is_ctx, scc_ref[...], scl_ref[0])
    ms = jnp.mean(xe * xe, axis=-1, keepdims=True)
    he = (xe * lax.rsqrt(ms + EPS) * ng_ref[...]) * (1.0 + sc) + sh
    hbe = he.astype(BF16)
    hb = he[SUBLANE:SUBLANE + tm].astype(BF16)

    ca, sa = ca_ref[...], sa_ref[...]
    cb, sb = cb_ref[...], sb_ref[...]

    zcq = _dot(hb, win_ref[:, _O_CQ:_O_CQ + _W_CQ])
    zckv = _dot(hb, win_ref[:, _O_CKV:_O_CKV + _W_CKV])
    zkr = _dot(hb, win_ref[:, _O_KR:_O_KR + _W_KR])
    zq = _dot(hb, win_ref[:, _O_GQ:_O_GQ + _W_GQ])
    zk = _dot(hb, win_ref[:, _O_GK2:_O_GK2 + _W_GK2])
    nv = MLA_HEADS * MLA_V
    vt_ref[0, nv:, :] = _dot_nt(wgvt_ref[...], hb).astype(BF16)
    zg = _dot(hbe, win_ref[:, _O_GDN:_O_GDN + _W_GDN])
    zb = _dot(hb, win_ref[:, _O_BX:_O_BX + _W_BX])
    za = _dot(hb, win_ref[:, _O_AX:_O_AX + _W_AX]) + dtb_ref[...]
    zgate = _dot(hb, win_ref[:, _O_GATE:_O_GATE + _W_GATE])

    cqn = zcq * lax.rsqrt(jnp.mean(zcq * zcq, axis=-1, keepdims=True) + EPS) * qng_ref[...]
    q = _dot(cqn.astype(BF16), wuq_ref[...]) * (MLA_SCALE * LOG2E)
    for h in range(MLA_HEADS):
        sl = slice(LANE * h, LANE * (h + 1))
        q_ref[0, h] = _rope_block(q[:, sl], ca, sa, MLA_ROPE // 4).astype(BF16)

    ckvn = (zckv * lax.rsqrt(jnp.mean(zckv * zckv, axis=-1, keepdims=True) + EPS) * kvng_ref[...]).astype(BF16)
    kr = _rope_block(zkr, ca, sa, MLA_ROPE // 4)
    kn = _dot(ckvn, wuk_ref[...])
    for h in range(MLA_HEADS):
        sl = slice(LANE * h, LANE * (h + 1))
        k_ref[0, h] = (kn[:, sl] + kr).astype(BF16)
    vt_ref[0, 0:nv, :] = _dot_nt(wuvt_ref[...], ckvn).astype(BF16)

    gsum = gsum_ref[...]
    qn = zq * lax.rsqrt(_group_sum(zq * zq, gsum) * (1.0 / GQA_DIM) + EPS) * gqg_ref[...]
    lane = lax.broadcasted_iota(jnp.int32, (tm, LANE), 1)
    for p in range(GQA_HEADS // 2):
        blk = _rope_block(qn[:, LANE * p:LANE * (p + 1)], cb, sb, GQA_DIM // 4)
        q_ref[0, MLA_HEADS + 2 * p] = jnp.where(lane < GQA_DIM, blk, 0.0).astype(BF16)
        q_ref[0, MLA_HEADS + 2 * p + 1] = jnp.where(lane >= GQA_DIM, blk, 0.0).astype(BF16)
    kn2 = zk * lax.rsqrt(_group_sum(zk * zk, gsum[:_W_GK2, :_W_GK2]) * (1.0 / GQA_DIM) + EPS) * gkg_ref[...]
    kblks = [_rope_block(kn2[:, LANE * p:LANE * (p + 1)], cb, sb, GQA_DIM // 4).astype(BF16) for p in range(2)]
    for j in range(GQA_HEADS):
        kv = j // (GQA_HEADS // GQA_KV_HEADS)
        k_ref[0, MLA_HEADS + j] = kblks[0 if (j % 2) == kv else 1]

    row = lax.broadcasted_iota(jnp.int32, (tm + 2 * SUBLANE, 1), 0)
    has_prev = jnp.logical_and(t > 0, jnp.logical_not(is_ctx))
    has_next = t < nt_lat - 1
    keep = jnp.logical_and(jnp.logical_or(row >= SUBLANE, has_prev),
                           jnp.logical_or(row < SUBLANE + tm, has_next))
    zg = jnp.where(keep, zg, 0.0)
    pad = GDN_CONV // 2
    y = jnp.zeros((tm, _W_GDN), F32)
    for j in range(GDN_CONV):
        o = SUBLANE - pad + j
        y = y + zg[o:o + tm] * cw_ref[j:j + 1, :]
    y = y * jax.nn.sigmoid(y)
    gs = gsum[:GDN_WIDTH, :GDN_WIDTH]
    qg = y[:, :GDN_WIDTH]
    kg = y[:, GDN_WIDTH:2 * GDN_WIDTH]
    qkv_ref[0, :, 0:GDN_WIDTH] = qg * lax.rsqrt(_group_sum(qg * qg, gs) + EPS) * (GDN_DK ** -0.5)
    qkv_ref[0, :, GDN_WIDTH:2 * GDN_WIDTH] = kg * lax.rsqrt(_group_sum(kg * kg, gs) + EPS)
    qkv_ref[0, :, 2 * GDN_WIDTH:] = y[:, 2 * GDN_WIDTH:]
    bgx_ref[0, :, 0:_W_BX] = jax.nn.sigmoid(zb)
    softplus = jnp.maximum(za, 0.0) + jnp.log(1.0 + jnp.exp(-jnp.abs(za)))
    bgx_ref[0, :, _W_BX:] = -jnp.exp(alog_ref[...]) * softplus

    sg_ref[0] = (zgate * jax.nn.sigmoid(zgate)).astype(BF16)


def _inproj(x, ctx, shl, scl, shc, scc, lw, tabs):
    B, L, D = x.shape
    nt_lat = L // TOK_TILE
    nt = nt_lat + CTX_LEN // TOK_TILE
    LT = L + CTX_LEN
    rb = TOK_TILE // SUBLANE
    nrb = L // SUBLANE
    full = lambda a: pl.BlockSpec(a.shape, lambda b, t: (0,) * a.ndim)
    tok = lambda w: pl.BlockSpec((1, TOK_TILE, w), lambda b, t: (b, t, 0))
    tab = pl.BlockSpec((TOK_TILE, LANE), lambda b, t: (t, 0))
    mod = pl.BlockSpec((1, 1, D), lambda b, t: (b, 0, 0))
    consts = [lw["norm_g"], lw["w_in"], lw["qn_g"], lw["w_uq"], lw["kvn_g"], lw["w_uk"], lw["w_uv_t"], lw["w_gv_t"],
              lw["gq_g"], lw["gk_g"], lw["conv_w"], lw["alog_x"], lw["dtb_x"]]
    in_specs = [
        pl.BlockSpec((1, TOK_TILE, D), lambda b, t: (b, jnp.minimum(t, nt_lat - 1), 0)),
        pl.BlockSpec((1, SUBLANE, D), lambda b, t: (b, jnp.clip(t * rb - 1, 0, nrb - 1), 0)),
        pl.BlockSpec((1, SUBLANE, D), lambda b, t: (b, jnp.clip((t + 1) * rb, 0, nrb - 1), 0)),
        pl.BlockSpec((1, CTX_LEN, D), lambda b, t: (b, 0, 0)),
        mod, mod, full(shc), full(scc),
    ] + [full(a) for a in consts] + [tab, tab, tab, tab, full(tabs["gsum"])]
    nh = MLA_HEADS + GQA_HEADS
    head_major = pl.BlockSpec((1, nh, TOK_TILE, LANE), lambda b, t: (b, 0, t, 0))
    out_w = [(_W_GDN, F32), (_W_BX + _W_AX, F32), (D_MIX, BF16)]
    out_specs = [head_major, head_major, pl.BlockSpec((1, nh * MLA_V, TOK_TILE), lambda b, t: (b, 0, t))
                 ] + [tok(w) for w, _ in out_w]
    out_shape = [jax.ShapeDtypeStruct((B, nh, LT, LANE), BF16), jax.ShapeDtypeStruct((B, nh, LT, LANE), BF16),
                 jax.ShapeDtypeStruct((B, nh * MLA_V, LT), BF16)
                 ] + [jax.ShapeDtypeStruct((B, LT, w), dt) for w, dt in out_w]
    return pl.pallas_call(
        functools.partial(_inproj_kernel, nt_lat),
        grid=(B, nt),
        in_specs=in_specs,
        out_specs=out_specs,
        out_shape=out_shape,
        compiler_params=_params("parallel", "arbitrary"),
        name="inproj",
    )(x, x, x, ctx, shl, scl, shc, scc, *consts,
      tabs["ca"], tabs["sa"], tabs["cb"], tabs["sb"], tabs["gsum"])


def _attn_kernel(q_ref, k_ref, vt_ref, o_ref, s0_scr, s1_scr, p0_scr, p1_scr):
    nh = q_ref.shape[1]
    tq = s0_scr.shape[1]
    n_items = nh * (q_ref.shape[2] // tq)
    assert n_items % 2 == 0
    dv = vt_ref.shape[1] // nh
    ones_rows = jnp.ones((2 * SUBLANE, k_ref.shape[2]), BF16)
    s_scr, p_scr = (s0_scr, s1_scr), (p0_scr, p1_scr)

    def scores(i, par):
        tile, h = i // nh, i % nh
        q = q_ref[0, h, pl.ds(pl.multiple_of(tile * tq, tq), tq), :]
        s = _dot_nt(k_ref[0, h], q)
        s_scr[par][...] = s
        return jnp.max(s, axis=0, keepdims=True)

    def probs(par, m):
        p_scr[par][...] = jnp.exp2(s_scr[par][...] - m).astype(BF16)

    def values(i, par):
        tile, h = i // nh, i % nh
        r0 = pl.multiple_of(h * dv, dv)
        lhs = jnp.concatenate([vt_ref[0, pl.ds(r0, dv), :], ones_rows], axis=0)
        r = _dot(lhs, p_scr[par][...])
        o_ref[0, tile, pl.ds(r0, dv), :] = (r[0:dv] / r[dv:dv + 1]).astype(o_ref.dtype)

    def stage(i, par, m_i):
        values(i - 1, 1 - par)
        m_next = scores(i + 1, 1 - par)
        probs(par, m_i)
        return m_next

    m0 = scores(0, 0)
    m1 = scores(1, 1)
    probs(0, m0)

    def body(j, m_odd):
        i = 2 * j + 1
        m_even = stage(i, 1, m_odd)
        return stage(i + 1, 0, m_even)

    m_last = lax.fori_loop(0, (n_items - 2) // 2, body, m1)
    values(n_items - 2, 0)
    probs(1, m_last)
    values(n_items - 1, 1)


def _attention(q, k, v_t, q_start, q_len, k_start, k_len):
    B, nh = q.shape[0], q.shape[1]
    tq = TOK_TILE
    tiles = min(ATTN_TILES_PER_STEP, q_len // tq)
    qb = tiles * tq
    nq = q_len // qb
    q0 = q_start // qb
    kblk = k_start // k_len
    wo = v_t.shape[1]
    return pl.pallas_call(
        _attn_kernel,
        grid=(B, nq),
        in_specs=[pl.BlockSpec((1, nh, qb, LANE), lambda b, i: (b, 0, q0 + i, 0)),
                  pl.BlockSpec((1, nh, k_len, LANE), lambda b, i: (b, 0, kblk, 0)),
                  pl.BlockSpec((1, wo, k_len), lambda b, i: (b, 0, kblk))],
        out_specs=pl.BlockSpec((1, tiles, wo, tq), lambda b, i: (b, i, 0, 0)),
        out_shape=jax.ShapeDtypeStruct((B, q_len // tq, wo, tq), BF16),
        scratch_shapes=[pltpu.VMEM((k_len, tq), F32), pltpu.VMEM((k_len, tq), F32),
                        pltpu.VMEM((k_len, tq), BF16), pltpu.VMEM((k_len, tq), BF16)],
        compiler_params=_params("parallel", "arbitrary"),
        name="attention",
    )(q, k, v_t)


def _gdn_kernel(n_lat, n_all, qkv_ref, beta_ref, g_ref, o_ref, mp_ref, qp_ref, cc_ref, gl_ref):
    d = pl.program_id(1)
    C, W = GDN_CHUNK, GDN_WIDTH
    sign = 1 - 2 * d

    ri = lax.broadcasted_iota(jnp.int32, (C, W), 0)
    cj = lax.broadcasted_iota(jnp.int32, (C, W), 1) % C
    incl = (ri - cj) * sign >= 0
    incl_t = jnp.where((cj - ri) * sign >= 0, 1.0, 0.0).astype(F32)
    diag = ri == cj
    blk4 = jnp.where(ri // 4 == cj // 4, 1.0, 0.0).astype(F32)
    lvl_masks = [jnp.where(jnp.logical_and(ri // (2 * bs) == cj // (2 * bs), ri // bs != cj // bs), 1.0, 0.0
                           ).astype(F32) for bs in (4, 8, 16, 32)]
    eye_lb = jnp.where(diag, 1.0, 0.0).astype(F32)
    r2 = lax.broadcasted_iota(jnp.int32, (C, C), 0)
    c2 = lax.broadcasted_iota(jnp.int32, (C, C), 1)
    tri = jnp.where((r2 - c2) * sign >= 0, 1.0, 0.0).astype(F32)
    ones8 = jnp.ones((SUBLANE, C), F32)
    rb = lax.broadcasted_iota(jnp.int32, (GDN_HEADS * C, W), 0) // C
    cbk = lax.broadcasted_iota(jnp.int32, (GDN_HEADS * C, W), 1) // C
    bmask = jnp.where(rb == cbk, 1.0, 0.0).astype(BF16)
    bmask_f = bmask.astype(F32)

    def bd(y):
        yb = y.astype(BF16)
        return jnp.concatenate([yb] * GDN_HEADS, axis=0) * bmask

    def mm(x, ybd):
        return _dot(x.astype(BF16), ybd)

    def diag_blocks(full):
        acc = full[0:C]
        for h in range(1, GDN_HEADS):
            acc = acc + full[C * h:C * (h + 1)]
        return acc

    @pl.when(d == 0)
    def _():
        o_ref[...] = jnp.zeros(o_ref.shape, o_ref.dtype)

    def phase1(it, carry):
        U = GDN_UNROLL
        each = lambda f, *ls: [f(*a) for a in zip(*ls)]
        r0s = [pl.multiple_of((it * U + c) * C, C) for c in range(U)]
        q = [qkv_ref[0, pl.ds(r0, C), 0:W] for r0 in r0s]
        k = [qkv_ref[0, pl.ds(r0, C), W:2 * W] for r0 in r0s]
        v = [qkv_ref[0, pl.ds(r0, C), 2 * W:3 * W] for r0 in r0s]
        beta = [beta_ref[0, pl.ds(r0, C), :] for r0 in r0s]
        g = [g_ref[0, pl.ds(r0, C), :] for r0 in r0s]
        gc = each(lambda x: jnp.dot(tri, x, precision=HIGHEST, preferred_element_type=F32), g)
        g2 = each(lambda x: jnp.dot(ones8, x * incl_t, precision=HIGHEST, preferred_element_type=F32), g)
        kbf = each(lambda x: x.astype(BF16), k)
        gram = each(lambda kb_, q_: _dot_nt(jnp.concatenate([kb_, q_.astype(BF16)], axis=0),
                                            jnp.concatenate([kb_] * GDN_HEADS, axis=0) * bmask), kbf, q)
        dlt = each(lambda a, b: a - jnp.broadcast_to(b[0:1], (C, W)), gc, g2)
        dec_incl = each(lambda x: jnp.where(incl, jnp.exp(jnp.where(incl, x, 0.0)), 0.0), dlt)
        gcl = each(lambda x: jnp.where(d == 0, x[C - 1:C], x[0:1]), gc)
        egc = each(jnp.exp, gc)
        a_mat = each(lambda gr, b, de: gr[0:C] * b * jnp.where(diag, 0.0, de), gram, beta, dec_incl)
        a_intra = each(lambda gr, de: (gr[C:2 * C] * de).astype(BF16), gram, dec_incl)

        b_0 = each(lambda a: -(a * blk4), a_mat)
        p_0 = each(lambda b: b + mm(b, bd(b)), b_0)
        t_mat = each(lambda b, p: eye_lb + b + mm(p, bd(b)), b_0, p_0)
        for msk in lvl_masks:
            y = each(lambda a, x: mm(a * msk, bd(x)), a_mat, t_mat)
            t_mat = each(lambda x, y_: x - mm(x, bd(y_)), t_mat, y)

        u = each(lambda t, v_, b: mm(t, bd(v_ * b)), t_mat, v, beta)
        w = each(lambda t, k_, b, e: mm(t, bd(k_ * b * e)), t_mat, k, beta, egc)
        kd_t = each(lambda k_, gl_, gc_: jnp.transpose(k_ * jnp.exp(gl_ - gc_)).astype(BF16), k, gcl, gc)
        kwu = each(lambda kt, w_, u_: _dot(kt, jnp.concatenate([w_, u_], axis=1).astype(BF16)), kd_t, w, u)
        aw = each(lambda a, w_: _dot(a, bd(w_)), a_intra, w)
        au = each(lambda a, u_: _dot(a, bd(u_)), a_intra, u)
        for c in range(U):
            n = it * U + c
            mp_ref[n] = (-diag_blocks(kwu[c][:, 0:W] * bmask_f)).astype(BF16)
            cc_ref[n] = diag_blocks(kwu[c][:, W:2 * W] * bmask_f)
            qp_ref[n] = (q[c] * egc[c] - aw[c]).astype(BF16)
            gl_ref[n] = jnp.broadcast_to(jnp.exp(gcl[c]), (SUBLANE, W))
            o_ref[0, pl.ds(r0s[c], C), :] += au[c]
        return carry

    assert n_all % GDN_UNROLL == 0
    lax.fori_loop(0, n_all // GDN_UNROLL, phase1, 0)

    n_ctx = n_all - n_lat

    def phase2(i, s):
        fwd = jnp.where(i < n_ctx, n_lat + i, i - n_ctx)
        n = jnp.where(d == 0, fwd, n_all - 1 - i)
        r0 = pl.multiple_of(n * C, C)
        r = _dot(jnp.concatenate([mp_ref[n], qp_ref[n]], axis=0), bd(s))
        o_ref[0, pl.ds(r0, C), :] += r[C:2 * C]
        return gl_ref[n][0:1] * s + r[0:C] + cc_ref[n]

    lax.fori_loop(0, n_all, phase2, jnp.zeros((C, W), F32))


def _gdn(qkv, bgx, n_lat_tokens):
    B, LT, _ = qkv.shape
    n_all = LT // GDN_CHUNK
    n_lat = n_lat_tokens // GDN_CHUNK
    W = GDN_WIDTH
    return pl.pallas_call(
        functools.partial(_gdn_kernel, n_lat, n_all),
        grid=(B, 2),
        in_specs=[
            pl.BlockSpec((1, LT, 3 * W), lambda b, d: (b, 0, 0)),
            pl.BlockSpec((1, LT, W), lambda b, d: (b, 0, d)),
            pl.BlockSpec((1, LT, W), lambda b, d: (b, 0, 2 + d)),
        ],
        out_specs=pl.BlockSpec((1, LT, W), lambda b, d: (b, 0, 0)),
        out_shape=jax.ShapeDtypeStruct((B, LT, W), F32),
        scratch_shapes=[
            pltpu.VMEM((n_all, GDN_CHUNK, W), BF16),
            pltpu.VMEM((n_all, GDN_CHUNK, W), BF16),
            pltpu.VMEM((n_all, GDN_CHUNK, W), F32),
            pltpu.VMEM((n_all, SUBLANE, W), F32),
        ],
        compiler_params=_params("parallel", "arbitrary"),
        name="gdn",
    )(qkv, bgx, bgx)


def _merge_kernel(final, x_ref, oa_ref, og_ref, sg_ref, gt_ref, ong_ref, gsum_ref, wout_ref, fg_ref, o_ref):
    wa = oa_ref.shape[2]
    og = og_ref[0]
    ogn = og * lax.rsqrt(_group_sum(og * og, gsum_ref[...]) * (1.0 / GDN_DV) + EPS) * ong_ref[...]
    ma = (jnp.transpose(oa_ref[0, 0].astype(F32)) * sg_ref[0, :, 0:wa].astype(F32)).astype(BF16)
    mg = (ogn * sg_ref[0, :, wa:].astype(F32)).astype(BF16)
    y = _dot(ma, wout_ref[0:wa, :]) + _dot(mg, wout_ref[wa:, :])
    xn = x_ref[0] + gt_ref[0] * y
    if final:
        xn = xn * lax.rsqrt(jnp.mean(xn * xn, axis=-1, keepdims=True) + EPS) * fg_ref[...]
    o_ref[0] = xn


def _merge(x, o_attn, o_gdn, sg, gt, lw, gsum, final_g, row_start, final):
    B, L, D = x.shape
    nt = L // TOK_TILE
    t0 = row_start // TOK_TILE
    full = lambda a: pl.BlockSpec(a.shape, lambda b, t: (0,) * a.ndim)
    return pl.pallas_call(
        functools.partial(_merge_kernel, final),
        grid=(B, nt),
        in_specs=[
            pl.BlockSpec((1, TOK_TILE, D), lambda b, t: (b, t, 0)),
            pl.BlockSpec((1, 1, o_attn.shape[2], TOK_TILE), lambda b, t: (b, t, 0, 0)),
            pl.BlockSpec((1, TOK_TILE, GDN_WIDTH), lambda b, t: (b, t0 + t, 0)),
            pl.BlockSpec((1, TOK_TILE, D_MIX), lambda b, t: (b, t0 + t, 0)),
            pl.BlockSpec((1, 1, D), lambda b, t: (b, 0, 0)),
            full(lw["on_g"]), full(gsum), full(lw["w_out"]), full(final_g),
        ],
        out_specs=pl.BlockSpec((1, TOK_TILE, D), lambda b, t: (b, t, 0)),
        out_shape=jax.ShapeDtypeStruct((B, L, D), F32),
        compiler_params=_params("parallel", "arbitrary"),
        name="merge",
    )(x, o_attn, o_gdn, sg, gt, lw["on_g"], gsum, lw["w_out"], final_g)


def _layer_weights(l, w_in, norm_g, mla_qn_g, mla_w_uq, mla_kvn_g, mla_w_ukv, gqa_qn_g, gqa_kn_g,
                   gdn_conv_w, gdn_a_log, gdn_dt_bias, gdn_on_g, w_out):
    D = D_MODEL
    w = w_in[l]
    splits = np.cumsum([0, MLA_Q_RANK, MLA_KV_RANK, MLA_ROPE, GQA_HEADS * GQA_DIM, GQA_KV_HEADS * GQA_DIM,
                        GQA_KV_HEADS * GQA_DIM, 3 * GDN_HEADS * GDN_DK, 2 * GDN_HEADS, 2 * GDN_HEADS, D_MIX])
    seg = [w[:, int(a):int(b)] for a, b in zip(splits[:-1], splits[1:])]
    cq, ckv, kr, gq, gk, gv, gdn, bb, aa, gate = seg
    z = lambda n: jnp.zeros((D, n), w.dtype)
    kr_p = jnp.concatenate([z(MLA_NOPE), kr, z(LANE - MLA_NOPE - MLA_ROPE)], axis=1)
    k0, k1 = gk[:, :GQA_DIM], gk[:, GQA_DIM:]
    gk2 = jnp.concatenate([k0, k1, k1, k0], axis=1)
    bx = jnp.repeat(bb, GDN_DV, axis=1)
    ax = jnp.repeat(aa, GDN_DV, axis=1)
    w_p = jnp.concatenate([cq, ckv, kr_p, gq, gk2, gdn, bx, ax, gate], axis=1).astype(BF16)

    dqk = MLA_NOPE + MLA_ROPE
    uq = mla_w_uq[l].reshape(MLA_Q_RANK, MLA_HEADS, dqk)
    uq_p = jnp.concatenate([uq, jnp.zeros((MLA_Q_RANK, MLA_HEADS, LANE - dqk), uq.dtype)], axis=2)
    ukv = mla_w_ukv[l].reshape(MLA_KV_RANK, MLA_HEADS, MLA_NOPE + MLA_V)
    uk_p = jnp.concatenate([ukv[:, :, :MLA_NOPE],
                            jnp.zeros((MLA_KV_RANK, MLA_HEADS, LANE - MLA_NOPE), ukv.dtype)], axis=2)
    uv = ukv[:, :, MLA_NOPE:]
    return {
        "norm_g": norm_g[l].reshape(1, D),
        "w_in": w_p,
        "qn_g": mla_qn_g[l].reshape(1, -1),
        "w_uq": uq_p.reshape(MLA_Q_RANK, MLA_HEADS * LANE).astype(BF16),
        "kvn_g": mla_kvn_g[l].reshape(1, -1),
        "w_uk": uk_p.reshape(MLA_KV_RANK, MLA_HEADS * LANE).astype(BF16),
        "w_uv_t": uv.reshape(MLA_KV_RANK, MLA_HEADS * MLA_V).T.astype(BF16),
        "w_gv_t": jnp.repeat(gv.T.reshape(GQA_KV_HEADS, GQA_DIM, D), GQA_HEADS // GQA_KV_HEADS, axis=0
                             ).reshape(GQA_HEADS * GQA_DIM, D).astype(BF16),
        "gq_g": (jnp.tile(gqa_qn_g[l], GQA_HEADS) * (GQA_SCALE * LOG2E)).reshape(1, -1),
        "gk_g": jnp.tile(gqa_kn_g[l], 2 * 2).reshape(1, -1),
        "conv_w": jnp.concatenate([gdn_conv_w[l], jnp.zeros((SUBLANE - GDN_CONV, _W_GDN), F32)], axis=0),
        "alog_x": jnp.repeat(gdn_a_log[l].reshape(-1), GDN_DV).reshape(1, -1),
        "dtb_x": jnp.repeat(gdn_dt_bias[l].reshape(-1), GDN_DV).reshape(1, -1),
        "on_g": jnp.tile(gdn_on_g[l], GDN_HEADS).reshape(1, -1),
        "w_out": w_out[l].astype(BF16),
    }


def _rope_tables(seq_len):
    LT = seq_len + CTX_LEN
    t = np.arange(seq_len)
    rows, cols = (t // GRID_W).astype(np.float64), (t % GRID_W).astype(np.float64)

    def fill(c, s, lane0, d, pos):
        half = d // 2
        inv = ROPE_BASE ** (-np.arange(0, d, 2, dtype=np.float64) / d)
        ang = pos[:, None] * inv[None, :]
        c[:seq_len, lane0:lane0 + half] = np.cos(ang)
        c[:seq_len, lane0 + half:lane0 + d] = np.cos(ang)
        s[:seq_len, lane0:lane0 + half] = -np.sin(ang)
        s[:seq_len, lane0 + half:lane0 + d] = np.sin(ang)

    ca, sa = np.ones((LT, LANE)), np.zeros((LT, LANE))
    fill(ca, sa, MLA_NOPE, MLA_ROPE // 2, rows)
    fill(ca, sa, MLA_NOPE + MLA_ROPE // 2, MLA_ROPE // 2, cols)
    cb, sb = np.ones((LT, LANE)), np.zeros((LT, LANE))
    for h0 in (0, GQA_DIM):
        fill(cb, sb, h0, GQA_DIM // 2, rows)
        fill(cb, sb, h0 + GQA_DIM // 2, GQA_DIM // 2, cols)
    lane = np.arange(3 * LANE)
    gsum = (lane[:, None] // GQA_DIM == lane[None, :] // GQA_DIM).astype(np.float32)
    return {"ca": jnp.asarray(ca, F32), "sa": jnp.asarray(sa, F32),
            "cb": jnp.asarray(cb, F32), "sb": jnp.asarray(sb, F32),
            "gsum": jnp.asarray(gsum, BF16)}


def kernel(x, c, ctx, c_ctx, w_ada, b_ada, norm_g, w_in, mla_qn_g, mla_w_uq, mla_kvn_g, mla_w_ukv,
           gqa_qn_g, gqa_kn_g, gdn_conv_w, gdn_a_log, gdn_dt_bias, gdn_on_g, w_out, final_g):
    B, L, D = x.shape
    assert D == D_MODEL and L % TOK_TILE == 0 and ctx.shape[1] == CTX_LEN
    tabs = _rope_tables(L)
    rows = -(-(B + 1) // SUBLANE) * SUBLANE
    cvec = jnp.concatenate([c, c_ctx[None, :], jnp.zeros((rows - B - 1, D), F32)], axis=0)
    mod = _adaln(cvec, w_ada, b_ada)
    fg = final_g.reshape(1, D)
    gs256 = tabs["gsum"][:GDN_WIDTH, :GDN_WIDTH]

    for l in range(DEPTH):
        lw = _layer_weights(l, w_in, norm_g, mla_qn_g, mla_w_uq, mla_kvn_g, mla_w_ukv, gqa_qn_g,
                            gqa_kn_g, gdn_conv_w, gdn_a_log, gdn_dt_bias, gdn_on_g, w_out)
        ml = mod[l, :B].reshape(B, 1, 3 * D)
        mc = mod[l, B].reshape(1, 3 * D)
        shl, scl, gtl = ml[..., :D], ml[..., D:2 * D], ml[..., 2 * D:]
        shc, scc, gtc = mc[:, :D], mc[:, D:2 * D], mc[:, 2 * D:]
        q, k, v_t, qkv, bgx, sg = _inproj(x, ctx, shl, scl, shc, scc, lw, tabs)
        o_gdn = _gdn(qkv, bgx, L)
        o_lat = _attention(q, k, v_t, 0, L, 0, L + CTX_LEN)
        last = l == DEPTH - 1
        x_new = _merge(x, o_lat, o_gdn, sg, gtl, lw, gs256, fg, 0, last)
        if not last:
            o_ctx = _attention(q, k, v_t, L, CTX_LEN, L, CTX_LEN)
            gtc_b = jnp.broadcast_to(gtc.reshape(1, 1, D), (B, 1, D))
            ctx = _merge(ctx, o_ctx, o_gdn, sg, gtc_b, lw, gs256, fg, L, False)
        x = x_new
    return x
```

```python
import functools
import math

import numpy as np
import jax
import jax.numpy as jnp
from jax import lax
from jax.experimental import pallas as pl
from jax.experimental.pallas import tpu as pltpu

F32 = jnp.float32
BF16 = jnp.bfloat16
HIGHEST = lax.Precision.HIGHEST

D_MODEL = 1024
DEPTH = 2
CTX_LEN = 256
GRID_W = 64
D_MIX = 1024
EPS = 1e-6
ROPE_BASE = 10000.0

MLA_HEADS = 6
MLA_NOPE = 64
MLA_ROPE = 32
MLA_V = 64
MLA_Q_RANK = 384
MLA_KV_RANK = 256
LOG2E = math.log2(math.e)
MLA_SCALE = (MLA_NOPE + MLA_ROPE) ** -0.5

GQA_HEADS = 6
GQA_KV_HEADS = 2
GQA_DIM = 64
GQA_SCALE = GQA_DIM ** -0.5

GDN_HEADS = 4
GDN_DK = 64
GDN_DV = 64
GDN_WIDTH = GDN_HEADS * GDN_DV
GDN_CONV = 5
GDN_CHUNK = 64

LANE = 128
SUBLANE = 8
TOK_TILE = 256
ATTN_TILES_PER_STEP = 4
GDN_UNROLL = 6
VMEM_LIMIT = 48 * 1024 * 1024

_W_CQ = MLA_Q_RANK
_W_CKV = MLA_KV_RANK
_W_KR = LANE
_W_GQ = GQA_HEADS * GQA_DIM
_W_GK2 = 2 * LANE
_W_GDN = 3 * GDN_WIDTH
_W_BA = LANE
_N_BA = 2 * GDN_HEADS
_W_GATE = D_MIX
_OFFS = np.cumsum([0, _W_CQ, _W_CKV, _W_KR, _W_GQ, _W_GK2, _W_GDN, _W_BA, _W_GATE])
(_O_CQ, _O_CKV, _O_KR, _O_GQ, _O_GK2, _O_GDN, _O_BA, _O_GATE, D_IN_P) = [int(v) for v in _OFFS]


def _dot(a, b):
    return jnp.dot(a, b, preferred_element_type=F32)


def _dot_nt(a, b):
    return lax.dot_general(a, b, (((1,), (1,)), ((), ())), preferred_element_type=F32)


def _params(*sem):
    return pltpu.CompilerParams(dimension_semantics=sem, vmem_limit_bytes=VMEM_LIMIT)


def _adaln_kernel(c_ref, w_ref, b_ref, o_ref):
    c = c_ref[...]
    s = c * jax.nn.sigmoid(c)
    o_ref[0] = jnp.dot(s, w_ref[0], precision=HIGHEST, preferred_element_type=F32) + b_ref[0]


def _adaln(cvec, w_ada, b_ada):
    rows = cvec.shape[0]
    nblk = w_ada.shape[2] // D_MODEL
    return pl.pallas_call(
        _adaln_kernel,
        grid=(DEPTH, nblk),
        in_specs=[
            pl.BlockSpec((rows, D_MODEL), lambda l, j: (0, 0)),
            pl.BlockSpec((1, D_MODEL, D_MODEL), lambda l, j: (l, 0, j)),
            pl.BlockSpec((1, 1, D_MODEL), lambda l, j: (l, 0, j)),
        ],
        out_specs=pl.BlockSpec((1, rows, D_MODEL), lambda l, j: (l, 0, j)),
        out_shape=jax.ShapeDtypeStruct((DEPTH, rows, 3 * D_MODEL), F32),
        compiler_params=_params("arbitrary", "arbitrary"),
        name="adaln",
    )(cvec, w_ada, b_ada.reshape(DEPTH, 1, 3 * D_MODEL))


def _group_sum(sq, gsum):
    return _dot(sq.astype(BF16), gsum)


def _rope_block(xb, c, s, half):
    lane = lax.broadcasted_iota(jnp.int32, xb.shape, 1)
    up = pltpu.roll(xb, LANE - half, 1)
    dn = pltpu.roll(xb, half, 1)
    partner = jnp.where((lane % (2 * half)) < half, up, dn)
    return xb * c + partner * s


def _inproj_kernel(nt_lat, x_ref, xp_ref, xn_ref, ctx_ref, shl_ref, scl_ref, shc_ref, scc_ref,
                   ng_ref, win_ref, qng_ref, wuq_ref, kvng_ref, wuk_ref, wuvt_ref, wgvt_ref,
                   gqg_ref, gkg_ref, cw_ref, alog_ref, dtb_ref, ca_ref, sa_ref, cb_ref, sb_ref,
                   gsum_ref,
                   q_ref, k_ref, vt_ref, qkv_ref, bg_ref, sg_ref):
    t = pl.program_id(1)
    is_ctx = t == nt_lat
    tm = TOK_TILE
    xt = jnp.where(is_ctx, ctx_ref[0], x_ref[0])
    xe = jnp.concatenate([xp_ref[0], xt, xn_ref[0]], axis=0)
    sh = jnp.where(is_ctx, shc_ref[...], shl_ref[0])
    sc = jnp.where(is_ctx, scc_ref[...], scl_ref[0])
    ms = jnp.mean(xe * xe, axis=-1, keepdims=True)
    he = (xe * lax.rsqrt(ms + EPS) * ng_ref[...]) * (1.0 + sc) + sh
    hbe = he.astype(BF16)
    hb = he[SUBLANE:SUBLANE + tm].astype(BF16)

    ca, sa = ca_ref[...], sa_ref[...]
    cb, sb = cb_ref[...], sb_ref[...]

    zg = _dot(hbe, win_ref[:, _O_GDN:_O_GDN + _W_GDN])
    zcq = _dot(hb, win_ref[:, _O_CQ:_O_CQ + _W_CQ])
    zckv = _dot(hb, win_ref[:, _O_CKV:_O_CKV + _W_CKV])
    zq = _dot(hb, win_ref[:, _O_GQ:_O_GQ + _W_GQ])
    zk = _dot(hb, win_ref[:, _O_GK2:_O_GK2 + _W_GK2])
    zkr = _dot(hb, win_ref[:, _O_KR:_O_KR + _W_KR])
    zba = _dot(hb, win_ref[:, _O_BA:_O_BA + _W_BA])
    nv = MLA_HEADS * MLA_V
    vt_ref[0, nv:, :] = _dot_nt(wgvt_ref[...], hb).astype(BF16)
    zgate = _dot(hb, win_ref[:, _O_GATE:_O_GATE + _W_GATE])

    gsum = gsum_ref[...]
    lane = lax.broadcasted_iota(jnp.int32, (tm, LANE), 1)

    row = lax.broadcasted_iota(jnp.int32, (tm + 2 * SUBLANE, 1), 0)
    has_prev = jnp.logical_and(t > 0, jnp.logical_not(is_ctx))
    has_next = t < nt_lat - 1
    keep = jnp.logical_and(jnp.logical_or(row >= SUBLANE, has_prev),
                           jnp.logical_or(row < SUBLANE + tm, has_next))
    zg = jnp.where(keep, zg, 0.0)
    pad = GDN_CONV // 2
    y = jnp.zeros((tm, _W_GDN), F32)
    for j in range(GDN_CONV):
        o = SUBLANE - pad + j
        y = y + zg[o:o + tm] * cw_ref[j:j + 1, :]
    y = y * jax.nn.sigmoid(y)
    gs = gsum[:GDN_WIDTH, :GDN_WIDTH]
    qg = y[:, :GDN_WIDTH]
    kg = y[:, GDN_WIDTH:2 * GDN_WIDTH]
    qkv_ref[0, :, 0:GDN_WIDTH] = qg * lax.rsqrt(_group_sum(qg * qg, gs) + EPS) * (GDN_DK ** -0.5)
    qkv_ref[0, :, GDN_WIDTH:2 * GDN_WIDTH] = kg * lax.rsqrt(_group_sum(kg * kg, gs) + EPS)
    qkv_ref[0, :, 2 * GDN_WIDTH:] = y[:, 2 * GDN_WIDTH:]
    za = zba + dtb_ref[...]
    softplus = jnp.maximum(za, 0.0) + jnp.log(1.0 + jnp.exp(-jnp.abs(za)))
    bg_ref[0] = jnp.where(lane < _N_BA, jax.nn.sigmoid(zba), -jnp.exp(alog_ref[...]) * softplus)

    cqn = zcq * lax.rsqrt(jnp.mean(zcq * zcq, axis=-1, keepdims=True) + EPS) * qng_ref[...]
    q = _dot(cqn.astype(BF16), wuq_ref[...]) * (MLA_SCALE * LOG2E)
    for h in range(MLA_HEADS):
        sl = slice(LANE * h, LANE * (h + 1))
        q_ref[0, h] = _rope_block(q[:, sl], ca, sa, MLA_ROPE // 4).astype(BF16)

    ckvn = (zckv * lax.rsqrt(jnp.mean(zckv * zckv, axis=-1, keepdims=True) + EPS) * kvng_ref[...]).astype(BF16)
    kr = _rope_block(zkr, ca, sa, MLA_ROPE // 4)
    kn = _dot(ckvn, wuk_ref[...])
    for h in range(MLA_HEADS):
        sl = slice(LANE * h, LANE * (h + 1))
        k_ref[0, h] = (kn[:, sl] + kr).astype(BF16)
    vt_ref[0, 0:nv, :] = _dot_nt(wuvt_ref[...], ckvn).astype(BF16)

    qn = zq * lax.rsqrt(_group_sum(zq * zq, gsum) * (1.0 / GQA_DIM) + EPS) * gqg_ref[...]
    for p in range(GQA_HEADS // 2):
        blk = _rope_block(qn[:, LANE * p:LANE * (p + 1)], cb, sb, GQA_DIM // 4)
        q_ref[0, MLA_HEADS + 2 * p] = jnp.where(lane < GQA_DIM, blk, 0.0).astype(BF16)
        q_ref[0, MLA_HEADS + 2 * p + 1] = jnp.where(lane >= GQA_DIM, blk, 0.0).astype(BF16)
    kn2 = zk * lax.rsqrt(_group_sum(zk * zk, gsum[:_W_GK2, :_W_GK2]) * (1.0 / GQA_DIM) + EPS) * gkg_ref[...]
    kblks = [_rope_block(kn2[:, LANE * p:LANE * (p + 1)], cb, sb, GQA_DIM // 4).astype(BF16) for p in range(2)]
    for j in range(GQA_HEADS):
        kv = j // (GQA_HEADS // GQA_KV_HEADS)
        k_ref[0, MLA_HEADS + j] = kblks[0 if (j % 2) == kv else 1]

    sg_ref[0] = (zgate * jax.nn.sigmoid(zgate)).astype(BF16)


def _inproj(x, ctx, shl, scl, shc, scc, lw, tabs):
    B, L, D = x.shape
    nt_lat = L // TOK_TILE
    nt = nt_lat + CTX_LEN // TOK_TILE
    LT = L + CTX_LEN
    rb = TOK_TILE // SUBLANE
    nrb = L // SUBLANE
    full = lambda a: pl.BlockSpec(a.shape, lambda b, t: (0,) * a.ndim)
    tok = lambda w: pl.BlockSpec((1, TOK_TILE, w), lambda b, t: (b, t, 0))
    tab = pl.BlockSpec((TOK_TILE, LANE), lambda b, t: (t, 0))
    mod = pl.BlockSpec((1, 1, D), lambda b, t: (b, 0, 0))
    consts = [lw["norm_g"], lw["w_in"], lw["qn_g"], lw["w_uq"], lw["kvn_g"], lw["w_uk"], lw["w_uv_t"], lw["w_gv_t"],
              lw["gq_g"], lw["gk_g"], lw["conv_w"], lw["alog_x"], lw["dtb_x"]]
    in_specs = [
        pl.BlockSpec((1, TOK_TILE, D), lambda b, t: (b, jnp.minimum(t, nt_lat - 1), 0)),
        pl.BlockSpec((1, SUBLANE, D), lambda b, t: (b, jnp.clip(t * rb - 1, 0, nrb - 1), 0)),
        pl.BlockSpec((1, SUBLANE, D), lambda b, t: (b, jnp.clip((t + 1) * rb, 0, nrb - 1), 0)),
        pl.BlockSpec((1, CTX_LEN, D), lambda b, t: (b, 0, 0)),
        mod, mod, full(shc), full(scc),
    ] + [full(a) for a in consts] + [tab, tab, tab, tab, full(tabs["gsum"])]
    nh = MLA_HEADS + GQA_HEADS
    head_major = pl.BlockSpec((1, nh, TOK_TILE, LANE), lambda b, t: (b, 0, t, 0))
    out_w = [(_W_GDN, F32), (_W_BA, F32), (D_MIX, BF16)]
    out_specs = [head_major, head_major, pl.BlockSpec((1, nh * MLA_V, TOK_TILE), lambda b, t: (b, 0, t))
                 ] + [tok(w) for w, _ in out_w]
    out_shape = [jax.ShapeDtypeStruct((B, nh, LT, LANE), BF16), jax.ShapeDtypeStruct((B, nh, LT, LANE), BF16),
                 jax.ShapeDtypeStruct((B, nh * MLA_V, LT), BF16)
                 ] + [jax.ShapeDtypeStruct((B, LT, w), dt) for w, dt in out_w]
    return pl.pallas_call(
        functools.partial(_inproj_kernel, nt_lat),
        grid=(B, nt),
        in_specs=in_specs,
        out_specs=out_specs,
        out_shape=out_shape,
        compiler_params=_params("parallel", "arbitrary"),
        name="inproj",
    )(x, x, x, ctx, shl, scl, shc, scc, *consts,
      tabs["ca"], tabs["sa"], tabs["cb"], tabs["sb"], tabs["gsum"])


def _attn_kernel(q_ref, k_ref, vt_ref, o_ref, s0_scr, s1_scr, p0_scr, p1_scr):
    nh = q_ref.shape[1]
    tq = s0_scr.shape[1]
    n_items = nh * (q_ref.shape[2] // tq)
    assert n_items % 2 == 0
    dv = vt_ref.shape[1] // nh
    ones_rows = jnp.ones((2 * SUBLANE, k_ref.shape[2]), BF16)
    s_scr, p_scr = (s0_scr, s1_scr), (p0_scr, p1_scr)

    def scores(i, par):
        tile, h = i // nh, i % nh
        q = q_ref[0, h, pl.ds(pl.multiple_of(tile * tq, tq), tq), :]
        s = _dot_nt(k_ref[0, h], q)
        s_scr[par][...] = s
        return jnp.max(s, axis=0, keepdims=True)

    def probs(par, m):
        p_scr[par][...] = jnp.exp2(s_scr[par][...] - m).astype(BF16)

    def values(i, par):
        tile, h = i // nh, i % nh
        r0 = pl.multiple_of(h * dv, dv)
        lhs = jnp.concatenate([vt_ref[0, pl.ds(r0, dv), :], ones_rows], axis=0)
        r = _dot(lhs, p_scr[par][...])
        o_ref[0, tile, pl.ds(r0, dv), :] = (r[0:dv] / r[dv:dv + 1]).astype(o_ref.dtype)

    def stage(i, par, m_i):
        values(i - 1, 1 - par)
        m_next = scores(i + 1, 1 - par)
        probs(par, m_i)
        return m_next

    m0 = scores(0, 0)
    m1 = scores(1, 1)
    probs(0, m0)

    def body(j, m_odd):
        i = 2 * j + 1
        m_even = stage(i, 1, m_odd)
        return stage(i + 1, 0, m_even)

    m_last = lax.fori_loop(0, (n_items - 2) // 2, body, m1)
    values(n_items - 2, 0)
    probs(1, m_last)
    values(n_items - 1, 1)


def _attention(q, k, v_t, q_start, q_len, k_start, k_len):
    B, nh = q.shape[0], q.shape[1]
    tq = TOK_TILE
    tiles = min(ATTN_TILES_PER_STEP, q_len // tq)
    qb = tiles * tq
    nq = q_len // qb
    q0 = q_start // qb
    kblk = k_start // k_len
    wo = v_t.shape[1]
    return pl.pallas_call(
        _attn_kernel,
        grid=(B, nq),
        in_specs=[pl.BlockSpec((1, nh, qb, LANE), lambda b, i: (b, 0, q0 + i, 0)),
                  pl.BlockSpec((1, nh, k_len, LANE), lambda b, i: (b, 0, kblk, 0)),
                  pl.BlockSpec((1, wo, k_len), lambda b, i: (b, 0, kblk))],
        out_specs=pl.BlockSpec((1, tiles, wo, tq), lambda b, i: (b, i, 0, 0)),
        out_shape=jax.ShapeDtypeStruct((B, q_len // tq, wo, tq), BF16),
        scratch_shapes=[pltpu.VMEM((k_len, tq), F32), pltpu.VMEM((k_len, tq), F32),
                        pltpu.VMEM((k_len, tq), BF16), pltpu.VMEM((k_len, tq), BF16)],
        compiler_params=_params("parallel", "arbitrary"),
        name="attention",
    )(q, k, v_t)


def _gdn_kernel(n_lat, n_all, qkv_ref, bg_ref, o_ref, mp_ref, qp_ref, cc_ref, gl_ref):
    C, W, U = GDN_CHUNK, GDN_WIDTH, GDN_UNROLL
    n_ctx = n_all - n_lat

    ri = lax.broadcasted_iota(jnp.int32, (C, W), 0)
    cj = lax.broadcasted_iota(jnp.int32, (C, W), 1) % C
    lane_head = lax.broadcasted_iota(jnp.int32, (C, W), 1) // GDN_DV
    diag = ri == cj
    eye_lb = jnp.where(diag, 1.0, 0.0).astype(F32)
    blk4 = jnp.where(ri // 4 == cj // 4, 1.0, 0.0).astype(F32)
    lvl_masks = [jnp.where(jnp.logical_and(ri // (2 * bs) == cj // (2 * bs), ri // bs != cj // bs), 1.0, 0.0
                           ).astype(F32) for bs in (4, 8, 16, 32)]
    r2 = lax.broadcasted_iota(jnp.int32, (C, C), 0)
    c2 = lax.broadcasted_iota(jnp.int32, (C, C), 1)
    ones8 = jnp.ones((SUBLANE, C), F32)
    rb = lax.broadcasted_iota(jnp.int32, (GDN_HEADS * C, W), 0) // C
    cbk = lax.broadcasted_iota(jnp.int32, (GDN_HEADS * C, W), 1) // C
    bmask = jnp.where(rb == cbk, 1.0, 0.0).astype(BF16)
    bmask_f = bmask.astype(F32)
    incl = [ri >= cj, ri <= cj]
    incl_t = [jnp.where(cj >= ri, 1.0, 0.0).astype(F32), jnp.where(cj <= ri, 1.0, 0.0).astype(F32)]
    tri = [jnp.where(r2 >= c2, 1.0, 0.0).astype(F32), jnp.where(r2 <= c2, 1.0, 0.0).astype(F32)]
    last_row = [C - 1, 0]

    def bd(y):
        yb = y.astype(BF16)
        return jnp.concatenate([yb] * GDN_HEADS, axis=0) * bmask

    def mm(x, ybd):
        return _dot(x.astype(BF16), ybd)

    def diag_blocks(full):
        acc = full[0:C]
        for h in range(1, GDN_HEADS):
            acc = acc + full[C * h:C * (h + 1)]
        return acc

    def expand(bg, col0):
        out = jnp.broadcast_to(bg[:, col0:col0 + 1], (C, W))
        for h in range(1, GDN_HEADS):
            out = jnp.where(lane_head == h, jnp.broadcast_to(bg[:, col0 + h:col0 + h + 1], (C, W)), out)
        return out

    def phase1(it, carry):
        probs = [(c, d) for c in range(U) for d in range(2)]
        each = lambda f, *ls: [f(*a) for a in zip(*ls)]
        per = lambda f: [f(c, d) for c, d in probs]
        rows = [pl.multiple_of((it * U + c) * C, C) for c in range(U)]
        q = [qkv_ref[0, pl.ds(r, C), 0:W] for r in rows]
        k = [qkv_ref[0, pl.ds(r, C), W:2 * W] for r in rows]
        v = [qkv_ref[0, pl.ds(r, C), 2 * W:3 * W] for r in rows]
        bg = [bg_ref[0, pl.ds(r, C), :] for r in rows]
        kbf = each(lambda x: x.astype(BF16), k)
        gram = each(lambda kb_, q_: _dot_nt(jnp.concatenate([kb_, q_.astype(BF16)], axis=0),
                                            jnp.concatenate([kb_] * GDN_HEADS, axis=0) * bmask), kbf, q)
        beta = per(lambda c, d: expand(bg[c], GDN_HEADS * d))
        g = per(lambda c, d: expand(bg[c], _N_BA + GDN_HEADS * d))
        dirs = [d for _, d in probs]
        chunk = [c for c, _ in probs]
        gc = each(lambda x, d: jnp.dot(tri[d], x, precision=HIGHEST, preferred_element_type=F32), g, dirs)
        g2 = each(lambda x, d: jnp.dot(ones8, x * incl_t[d], precision=HIGHEST, preferred_element_type=F32), g, dirs)
        dlt = each(lambda a, b: a - jnp.broadcast_to(b[0:1], (C, W)), gc, g2)
        dec = each(lambda x, d: jnp.where(incl[d], jnp.exp(jnp.where(incl[d], x, 0.0)), 0.0), dlt, dirs)
        gcl = each(lambda x, d: x[last_row[d]:last_row[d] + 1], gc, dirs)
        egc = each(jnp.exp, gc)
        a_mat = each(lambda c, b, de: gram[c][0:C] * b * jnp.where(diag, 0.0, de), chunk, beta, dec)
        a_intra = each(lambda c, de: (gram[c][C:2 * C] * de).astype(BF16), chunk, dec)

        b_0 = each(lambda a: -(a * blk4), a_mat)
        p_0 = each(lambda b: b + mm(b, bd(b)), b_0)
        t_mat = each(lambda b, p: eye_lb + b + mm(p, bd(b)), b_0, p_0)
        for msk in lvl_masks:
            y = each(lambda a, x: mm(a * msk, bd(x)), a_mat, t_mat)
            t_mat = each(lambda x, y_: x - mm(x, bd(y_)), t_mat, y)

        u = each(lambda t, c, b: mm(t, bd(v[c] * b)), t_mat, chunk, beta)
        w = each(lambda t, c, b, e: mm(t, bd(k[c] * b * e)), t_mat, chunk, beta, egc)
        kd_t = each(lambda c, gl_, gc_: jnp.transpose(k[c] * jnp.exp(gl_ - gc_)).astype(BF16), chunk, gcl, gc)
        kwu = each(lambda kt, w_, u_: _dot(kt, jnp.concatenate([w_, u_], axis=1).astype(BF16)), kd_t, w, u)
        aw = each(lambda a, w_: _dot(a, bd(w_)), a_intra, w)
        au = each(lambda a, u_: _dot(a, bd(u_)), a_intra, u)
        for p, (c, d) in enumerate(probs):
            n = it * U + c
            mp_ref[d, n] = (-diag_blocks(kwu[p][:, 0:W] * bmask_f)).astype(BF16)
            cc_ref[d, n] = diag_blocks(kwu[p][:, W:2 * W] * bmask_f)
            qp_ref[d, n] = (q[c] * egc[p] - aw[p]).astype(BF16)
            gl_ref[d, n] = jnp.broadcast_to(jnp.exp(gcl[p]), (SUBLANE, W))
        for c in range(U):
            o_ref[0, pl.ds(rows[c], C), :] = au[2 * c] + au[2 * c + 1]
        return carry

    assert n_all % U == 0
    lax.fori_loop(0, n_all // U, phase1, 0)

    def phase2(i, carry):
        n = (jnp.where(i < n_ctx, n_lat + i, i - n_ctx), n_all - 1 - i)
        r = [_dot(jnp.concatenate([mp_ref[d, n[d]], qp_ref[d, n[d]]], axis=0), bd(carry[d])) for d in range(2)]
        new = []
        for d in range(2):
            r0 = pl.multiple_of(n[d] * C, C)
            o_ref[0, pl.ds(r0, C), :] += r[d][C:2 * C]
            new.append(gl_ref[d, n[d]][0:1] * carry[d] + r[d][0:C] + cc_ref[d, n[d]])
        return tuple(new)

    zero = jnp.zeros((C, W), F32)
    lax.fori_loop(0, n_all, phase2, (zero, zero))


def _gdn(qkv, bg, n_lat_tokens):
    B, LT, _ = qkv.shape
    n_all = LT // GDN_CHUNK
    n_lat = n_lat_tokens // GDN_CHUNK
    W = GDN_WIDTH
    return pl.pallas_call(
        functools.partial(_gdn_kernel, n_lat, n_all),
        grid=(B,),
        in_specs=[
            pl.BlockSpec((1, LT, 3 * W), lambda b: (b, 0, 0)),
            pl.BlockSpec((1, LT, LANE), lambda b: (b, 0, 0)),
        ],
        out_specs=pl.BlockSpec((1, LT, W), lambda b: (b, 0, 0)),
        out_shape=jax.ShapeDtypeStruct((B, LT, W), F32),
        scratch_shapes=[
            pltpu.VMEM((2, n_all, GDN_CHUNK, W), BF16),
            pltpu.VMEM((2, n_all, GDN_CHUNK, W), BF16),
            pltpu.VMEM((2, n_all, GDN_CHUNK, W), F32),
            pltpu.VMEM((2, n_all, SUBLANE, W), F32),
        ],
        compiler_params=_params("parallel"),
        name="gdn",
    )(qkv, bg)


def _merge_kernel(final, x_ref, oa_ref, og_ref, sg_ref, gt_ref, ong_ref, gsum_ref, wout_ref, fg_ref, o_ref):
    wa = oa_ref.shape[2]
    og = og_ref[0]
    ogn = og * lax.rsqrt(_group_sum(og * og, gsum_ref[...]) * (1.0 / GDN_DV) + EPS) * ong_ref[...]
    ma = (jnp.transpose(oa_ref[0, 0].astype(F32)) * sg_ref[0, :, 0:wa].astype(F32)).astype(BF16)
    mg = (ogn * sg_ref[0, :, wa:].astype(F32)).astype(BF16)
    y = _dot(ma, wout_ref[0:wa, :]) + _dot(mg, wout_ref[wa:, :])
    xn = x_ref[0] + gt_ref[0] * y
    if final:
        xn = xn * lax.rsqrt(jnp.mean(xn * xn, axis=-1, keepdims=True) + EPS) * fg_ref[...]
    o_ref[0] = xn


def _merge(x, o_attn, o_gdn, sg, gt, lw, gsum, final_g, row_start, final):
    B, L, D = x.shape
    nt = L // TOK_TILE
    t0 = row_start // TOK_TILE
    full = lambda a: pl.BlockSpec(a.shape, lambda b, t: (0,) * a.ndim)
    return pl.pallas_call(
        functools.partial(_merge_kernel, final),
        grid=(B, nt),
        in_specs=[
            pl.BlockSpec((1, TOK_TILE, D), lambda b, t: (b, t, 0)),
            pl.BlockSpec((1, 1, o_attn.shape[2], TOK_TILE), lambda b, t: (b, t, 0, 0)),
            pl.BlockSpec((1, TOK_TILE, GDN_WIDTH), lambda b, t: (b, t0 + t, 0)),
            pl.BlockSpec((1, TOK_TILE, D_MIX), lambda b, t: (b, t0 + t, 0)),
            pl.BlockSpec((1, 1, D), lambda b, t: (b, 0, 0)),
            full(lw["on_g"]), full(gsum), full(lw["w_out"]), full(final_g),
        ],
        out_specs=pl.BlockSpec((1, TOK_TILE, D), lambda b, t: (b, t, 0)),
        out_shape=jax.ShapeDtypeStruct((B, L, D), F32),
        compiler_params=_params("parallel", "arbitrary"),
        name="merge",
    )(x, o_attn, o_gdn, sg, gt, lw["on_g"], gsum, lw["w_out"], final_g)


def _layer_weights(l, w_in, norm_g, mla_qn_g, mla_w_uq, mla_kvn_g, mla_w_ukv, gqa_qn_g, gqa_kn_g,
                   gdn_conv_w, gdn_a_log, gdn_dt_bias, gdn_on_g, w_out):
    D = D_MODEL
    w = w_in[l]
    splits = np.cumsum([0, MLA_Q_RANK, MLA_KV_RANK, MLA_ROPE, GQA_HEADS * GQA_DIM, GQA_KV_HEADS * GQA_DIM,
                        GQA_KV_HEADS * GQA_DIM, 3 * GDN_HEADS * GDN_DK, 2 * GDN_HEADS, 2 * GDN_HEADS, D_MIX])
    seg = [w[:, int(a):int(b)] for a, b in zip(splits[:-1], splits[1:])]
    cq, ckv, kr, gq, gk, gv, gdn, bb, aa, gate = seg
    z = lambda n: jnp.zeros((D, n), w.dtype)
    kr_p = jnp.concatenate([z(MLA_NOPE), kr, z(LANE - MLA_NOPE - MLA_ROPE)], axis=1)
    k0, k1 = gk[:, :GQA_DIM], gk[:, GQA_DIM:]
    gk2 = jnp.concatenate([k0, k1, k1, k0], axis=1)
    ba = jnp.concatenate([bb, aa, z(LANE - 2 * _N_BA)], axis=1)
    w_p = jnp.concatenate([cq, ckv, kr_p, gq, gk2, gdn, ba, gate], axis=1).astype(BF16)

    dqk = MLA_NOPE + MLA_ROPE
    uq = mla_w_uq[l].reshape(MLA_Q_RANK, MLA_HEADS, dqk)
    uq_p = jnp.concatenate([uq, jnp.zeros((MLA_Q_RANK, MLA_HEADS, LANE - dqk), uq.dtype)], axis=2)
    ukv = mla_w_ukv[l].reshape(MLA_KV_RANK, MLA_HEADS, MLA_NOPE + MLA_V)
    uk_p = jnp.concatenate([ukv[:, :, :MLA_NOPE],
                            jnp.zeros((MLA_KV_RANK, MLA_HEADS, LANE - MLA_NOPE), ukv.dtype)], axis=2)
    uv = ukv[:, :, MLA_NOPE:]
    return {
        "norm_g": norm_g[l].reshape(1, D),
        "w_in": w_p,
        "qn_g": mla_qn_g[l].reshape(1, -1),
        "w_uq": uq_p.reshape(MLA_Q_RANK, MLA_HEADS * LANE).astype(BF16),
        "kvn_g": mla_kvn_g[l].reshape(1, -1),
        "w_uk": uk_p.reshape(MLA_KV_RANK, MLA_HEADS * LANE).astype(BF16),
        "w_uv_t": uv.reshape(MLA_KV_RANK, MLA_HEADS * MLA_V).T.astype(BF16),
        "w_gv_t": jnp.repeat(gv.T.reshape(GQA_KV_HEADS, GQA_DIM, D), GQA_HEADS // GQA_KV_HEADS, axis=0
                             ).reshape(GQA_HEADS * GQA_DIM, D).astype(BF16),
        "gq_g": (jnp.tile(gqa_qn_g[l], GQA_HEADS) * (GQA_SCALE * LOG2E)).reshape(1, -1),
        "gk_g": jnp.tile(gqa_kn_g[l], 2 * 2).reshape(1, -1),
        "conv_w": jnp.concatenate([gdn_conv_w[l], jnp.zeros((SUBLANE - GDN_CONV, _W_GDN), F32)], axis=0),
        "alog_x": jnp.concatenate([jnp.zeros((_N_BA,), F32), gdn_a_log[l].reshape(-1),
                                   jnp.zeros((LANE - 2 * _N_BA,), F32)]).reshape(1, LANE),
        "dtb_x": jnp.concatenate([jnp.zeros((_N_BA,), F32), gdn_dt_bias[l].reshape(-1),
                                  jnp.zeros((LANE - 2 * _N_BA,), F32)]).reshape(1, LANE),
        "on_g": jnp.tile(gdn_on_g[l], GDN_HEADS).reshape(1, -1),
        "w_out": w_out[l].astype(BF16),
    }


def _rope_tables(seq_len):
    LT = seq_len + CTX_LEN
    t = np.arange(seq_len)
    rows, cols = (t // GRID_W).astype(np.float64), (t % GRID_W).astype(np.float64)

    def fill(c, s, lane0, d, pos):
        half = d // 2
        inv = ROPE_BASE ** (-np.arange(0, d, 2, dtype=np.float64) / d)
        ang = pos[:, None] * inv[None, :]
        c[:seq_len, lane0:lane0 + half] = np.cos(ang)
        c[:seq_len, lane0 + half:lane0 + d] = np.cos(ang)
        s[:seq_len, lane0:lane0 + half] = -np.sin(ang)
        s[:seq_len, lane0 + half:lane0 + d] = np.sin(ang)

    ca, sa = np.ones((LT, LANE)), np.zeros((LT, LANE))
    fill(ca, sa, MLA_NOPE, MLA_ROPE // 2, rows)
    fill(ca, sa, MLA_NOPE + MLA_ROPE // 2, MLA_ROPE // 2, cols)
    cb, sb = np.ones((LT, LANE)), np.zeros((LT, LANE))
    for h0 in (0, GQA_DIM):
        fill(cb, sb, h0, GQA_DIM // 2, rows)
        fill(cb, sb, h0 + GQA_DIM // 2, GQA_DIM // 2, cols)
    lane = np.arange(3 * LANE)
    gsum = (lane[:, None] // GQA_DIM == lane[None, :] // GQA_DIM).astype(np.float32)
    return {"ca": jnp.asarray(ca, F32), "sa": jnp.asarray(sa, F32),
            "cb": jnp.asarray(cb, F32), "sb": jnp.asarray(sb, F32),
            "gsum": jnp.asarray(gsum, BF16)}


def kernel(x, c, ctx, c_ctx, w_ada, b_ada, norm_g, w_in, mla_qn_g, mla_w_uq, mla_kvn_g, mla_w_ukv,
           gqa_qn_g, gqa_kn_g, gdn_conv_w, gdn_a_log, gdn_dt_bias, gdn_on_g, w_out, final_g):
    B, L, D = x.shape
    assert D == D_MODEL and L % TOK_TILE == 0 and ctx.shape[1] == CTX_LEN
    tabs = _rope_tables(L)
    rows = -(-(B + 1) // SUBLANE) * SUBLANE
    cvec = jnp.concatenate([c, c_ctx[None, :], jnp.zeros((rows - B - 1, D), F32)], axis=0)
    mod = _adaln(cvec, w_ada, b_ada)
    fg = final_g.reshape(1, D)
    gs256 = tabs["gsum"][:GDN_WIDTH, :GDN_WIDTH]

    for l in range(DEPTH):
        lw = _layer_weights(l, w_in, norm_g, mla_qn_g, mla_w_uq, mla_kvn_g, mla_w_ukv, gqa_qn_g,
                            gqa_kn_g, gdn_conv_w, gdn_a_log, gdn_dt_bias, gdn_on_g, w_out)
        ml = mod[l, :B].reshape(B, 1, 3 * D)
        mc = mod[l, B].reshape(1, 3 * D)
        shl, scl, gtl = ml[..., :D], ml[..., D:2 * D], ml[..., 2 * D:]
        shc, scc, gtc = mc[:, :D], mc[:, D:2 * D], mc[:, 2 * D:]
        q, k, v_t, qkv, bg, sg = _inproj(x, ctx, shl, scl, shc, scc, lw, tabs)
        o_gdn = _gdn(qkv, bg, L)
        o_lat = _attention(q, k, v_t, 0, L, 0, L + CTX_LEN)
        last = l == DEPTH - 1
        x_new = _merge(x, o_lat, o_gdn, sg, gtl, lw, gs256, fg, 0, last)
        if not last:
            o_ctx = _attention(q, k, v_t, L, CTX_LEN, L, CTX_LEN)
            gtc_b = jnp.broadcast_to(gtc.reshape(1, 1, D), (B, 1, D))
            ctx = _merge(ctx, o_ctx, o_gdn, sg, gtc_b, lw, gs256, fg, L, False)
        x = x_new
    return x
```

```python
import functools
import math

import numpy as np
import jax
import jax.numpy as jnp
from jax import lax
from jax.experimental import pallas as pl
from jax.experimental.pallas import tpu as pltpu

F32 = jnp.float32
BF16 = jnp.bfloat16
HIGHEST = lax.Precision.HIGHEST

D_MODEL = 1024
DEPTH = 2
CTX_LEN = 256
GRID_W = 64
D_MIX = 1024
EPS = 1e-6
ROPE_BASE = 10000.0

MLA_HEADS = 6
MLA_NOPE = 64
MLA_ROPE = 32
MLA_V = 64
MLA_Q_RANK = 384
MLA_KV_RANK = 256
LOG2E = math.log2(math.e)
MLA_SCALE = (MLA_NOPE + MLA_ROPE) ** -0.5

GQA_HEADS = 6
GQA_KV_HEADS = 2
GQA_DIM = 64
GQA_SCALE = GQA_DIM ** -0.5

GDN_HEADS = 4
GDN_DK = 64
GDN_DV = 64
GDN_WIDTH = GDN_HEADS * GDN_DV
GDN_CONV = 5
GDN_CHUNK = 64

LANE = 128
SUBLANE = 8
TOK_TILE = 256
ATTN_TILES_PER_STEP = 4
INPROJ_SUB = 3
MERGE_SUB = 4
GDN_UNROLL = 6
VMEM_LIMIT = 48 * 1024 * 1024

_W_CQ = MLA_Q_RANK
_W_CKV = MLA_KV_RANK
_W_KR = LANE
_W_GQ = GQA_HEADS * GQA_DIM
_W_GK2 = 2 * LANE
_W_GDN = 3 * GDN_WIDTH
_W_BA = LANE
_N_BA = 2 * GDN_HEADS
_W_GATE = D_MIX
_OFFS = np.cumsum([0, _W_CQ, _W_CKV, _W_KR, _W_GQ, _W_GK2, _W_GDN, _W_BA, _W_GATE])
(_O_CQ, _O_CKV, _O_KR, _O_GQ, _O_GK2, _O_GDN, _O_BA, _O_GATE, D_IN_P) = [int(v) for v in _OFFS]


def _dot(a, b):
    return jnp.dot(a, b, preferred_element_type=F32)


def _dot_nt(a, b):
    return lax.dot_general(a, b, (((1,), (1,)), ((), ())), preferred_element_type=F32)


def _params(*sem):
    return pltpu.CompilerParams(dimension_semantics=sem, vmem_limit_bytes=VMEM_LIMIT)


def _adaln_kernel(c_ref, w_ref, b_ref, o_ref):
    c = c_ref[...]
    s = c * jax.nn.sigmoid(c)
    o_ref[0] = jnp.dot(s, w_ref[0], precision=HIGHEST, preferred_element_type=F32) + b_ref[0]


def _adaln(cvec, w_ada, b_ada):
    rows = cvec.shape[0]
    nblk = w_ada.shape[2] // D_MODEL
    return pl.pallas_call(
        _adaln_kernel,
        grid=(DEPTH, nblk),
        in_specs=[
            pl.BlockSpec((rows, D_MODEL), lambda l, j: (0, 0)),
            pl.BlockSpec((1, D_MODEL, D_MODEL), lambda l, j: (l, 0, j)),
            pl.BlockSpec((1, 1, D_MODEL), lambda l, j: (l, 0, j)),
        ],
        out_specs=pl.BlockSpec((1, rows, D_MODEL), lambda l, j: (l, 0, j)),
        out_shape=jax.ShapeDtypeStruct((DEPTH, rows, 3 * D_MODEL), F32),
        compiler_params=_params("arbitrary", "arbitrary"),
        name="adaln",
    )(cvec, w_ada, b_ada.reshape(DEPTH, 1, 3 * D_MODEL))


def _group_sum(sq, gsum):
    return _dot(sq.astype(BF16), gsum)


def _rope_block(xb, c, s, half):
    lane = lax.broadcasted_iota(jnp.int32, xb.shape, 1)
    up = pltpu.roll(xb, LANE - half, 1)
    dn = pltpu.roll(xb, half, 1)
    partner = jnp.where((lane % (2 * half)) < half, up, dn)
    return xb * c + partner * s


def _inproj_kernel(nt_lat, ns, *refs):
    x_refs, xp_refs, xn_refs = refs[0:ns], refs[ns:2 * ns], refs[2 * ns:3 * ns]
    (ctx_ref, shl_ref, scl_ref, shc_ref, scc_ref, ng_ref, win_ref, qng_ref, wuq_ref, kvng_ref, wuk_ref,
     wuvt_ref, wgvt_ref, gqg_ref, gkg_ref, cw_ref, alog_ref, dtb_ref, ca_ref, sa_ref, cb_ref, sb_ref,
     gsum_ref, q_ref, k_ref, vt_ref, qkv_ref, bg_ref, sg_ref) = refs[3 * ns:]
    t = pl.program_id(1)
    tm = TOK_TILE
    nv = MLA_HEADS * MLA_V
    gsum = gsum_ref[...]
    lane = lax.broadcasted_iota(jnp.int32, (tm, LANE), 1)
    row = lax.broadcasted_iota(jnp.int32, (tm + 2 * SUBLANE, 1), 0)
    tiles = [ns * t + s for s in range(ns)]
    rows = [slice(tm * s, tm * (s + 1)) for s in range(ns)]

    hbe, hb = [], []
    for s in range(ns):
        is_ctx = tiles[s] == nt_lat
        xt = jnp.where(is_ctx, ctx_ref[0], x_refs[s][0])
        xe = jnp.concatenate([xp_refs[s][0], xt, xn_refs[s][0]], axis=0)
        sh = jnp.where(is_ctx, shc_ref[...], shl_ref[0])
        sc = jnp.where(is_ctx, scc_ref[...], scl_ref[0])
        ms = jnp.mean(xe * xe, axis=-1, keepdims=True)
        he = (xe * lax.rsqrt(ms + EPS) * ng_ref[...]) * (1.0 + sc) + sh
        hbe.append(he.astype(BF16))
        hb.append(he[SUBLANE:SUBLANE + tm].astype(BF16))

    z = []
    for s in range(ns):
        zs = dict(
            g=_dot(hbe[s], win_ref[:, _O_GDN:_O_GDN + _W_GDN]),
            cq=_dot(hb[s], win_ref[:, _O_CQ:_O_CQ + _W_CQ]),
            ckv=_dot(hb[s], win_ref[:, _O_CKV:_O_CKV + _W_CKV]),
            q=_dot(hb[s], win_ref[:, _O_GQ:_O_GQ + _W_GQ]),
            k=_dot(hb[s], win_ref[:, _O_GK2:_O_GK2 + _W_GK2]),
            kr=_dot(hb[s], win_ref[:, _O_KR:_O_KR + _W_KR]),
            ba=_dot(hb[s], win_ref[:, _O_BA:_O_BA + _W_BA]))
        vg = _dot_nt(wgvt_ref[...], hb[s]).astype(BF16)
        for j in range(GQA_HEADS):
            kv = j // (GQA_HEADS // GQA_KV_HEADS)
            vt_ref[0, nv + GQA_DIM * j:nv + GQA_DIM * (j + 1), rows[s]] = vg[GQA_DIM * kv:GQA_DIM * (kv + 1)]
        zs["gate"] = _dot(hb[s], win_ref[:, _O_GATE:_O_GATE + _W_GATE])
        z.append(zs)

    for s in range(ns):
        zs, rs = z[s], rows[s]
        is_ctx = tiles[s] == nt_lat
        ca, sa = ca_ref[rs, :], sa_ref[rs, :]
        cb, sb = cb_ref[rs, :], sb_ref[rs, :]

        has_prev = jnp.logical_and(tiles[s] > 0, jnp.logical_not(is_ctx))
        has_next = tiles[s] < nt_lat - 1
        keep = jnp.logical_and(jnp.logical_or(row >= SUBLANE, has_prev),
                               jnp.logical_or(row < SUBLANE + tm, has_next))
        zg = jnp.where(keep, zs["g"], 0.0)
        pad = GDN_CONV // 2
        y = jnp.zeros((tm, _W_GDN), F32)
        for j in range(GDN_CONV):
            o = SUBLANE - pad + j
            y = y + zg[o:o + tm] * cw_ref[j:j + 1, :]
        y = y * jax.nn.sigmoid(y)
        gs = gsum[:GDN_WIDTH, :GDN_WIDTH]
        qg = y[:, :GDN_WIDTH]
        kg = y[:, GDN_WIDTH:2 * GDN_WIDTH]
        qkv_ref[0, rs, 0:GDN_WIDTH] = qg * lax.rsqrt(_group_sum(qg * qg, gs) + EPS) * (GDN_DK ** -0.5)
        qkv_ref[0, rs, GDN_WIDTH:2 * GDN_WIDTH] = kg * lax.rsqrt(_group_sum(kg * kg, gs) + EPS)
        qkv_ref[0, rs, 2 * GDN_WIDTH:] = y[:, 2 * GDN_WIDTH:]
        za = zs["ba"] + dtb_ref[...]
        softplus = jnp.maximum(za, 0.0) + jnp.log(1.0 + jnp.exp(-jnp.abs(za)))
        bg_ref[0, rs, :] = jnp.where(lane < _N_BA, jax.nn.sigmoid(zs["ba"]), -jnp.exp(alog_ref[...]) * softplus)

        zcq = zs["cq"]
        cqn = zcq * lax.rsqrt(jnp.mean(zcq * zcq, axis=-1, keepdims=True) + EPS) * qng_ref[...]
        q = _dot(cqn.astype(BF16), wuq_ref[...]) * (MLA_SCALE * LOG2E)
        for h in range(MLA_HEADS):
            sl = slice(LANE * h, LANE * (h + 1))
            q_ref[0, h, rs, :] = _rope_block(q[:, sl], ca, sa, MLA_ROPE // 4).astype(BF16)

        zckv = zs["ckv"]
        ckvn = (zckv * lax.rsqrt(jnp.mean(zckv * zckv, axis=-1, keepdims=True) + EPS) * kvng_ref[...]).astype(BF16)
        kr = _rope_block(zs["kr"], ca, sa, MLA_ROPE // 4)
        kn = _dot(ckvn, wuk_ref[...])
        for h in range(MLA_HEADS):
            sl = slice(LANE * h, LANE * (h + 1))
            k_ref[0, h, rs, :] = (kn[:, sl] + kr).astype(BF16)
        vt_ref[0, 0:nv, rs] = _dot_nt(wuvt_ref[...], ckvn).astype(BF16)

        zq, zk = zs["q"], zs["k"]
        qn = zq * lax.rsqrt(_group_sum(zq * zq, gsum) * (1.0 / GQA_DIM) + EPS) * gqg_ref[...]
        for p in range(GQA_HEADS // 2):
            blk = _rope_block(qn[:, LANE * p:LANE * (p + 1)], cb, sb, GQA_DIM // 4)
            q_ref[0, MLA_HEADS + 2 * p, rs, :] = jnp.where(lane < GQA_DIM, blk, 0.0).astype(BF16)
            q_ref[0, MLA_HEADS + 2 * p + 1, rs, :] = jnp.where(lane >= GQA_DIM, blk, 0.0).astype(BF16)
        kn2 = zk * lax.rsqrt(_group_sum(zk * zk, gsum[:_W_GK2, :_W_GK2]) * (1.0 / GQA_DIM) + EPS) * gkg_ref[...]
        kblks = [_rope_block(kn2[:, LANE * p:LANE * (p + 1)], cb, sb, GQA_DIM // 4).astype(BF16) for p in range(2)]
        for j in range(GQA_HEADS):
            kv = j // (GQA_HEADS // GQA_KV_HEADS)
            k_ref[0, MLA_HEADS + j, rs, :] = kblks[0 if (j % 2) == kv else 1]

        zgate = zs["gate"]
        sg_ref[0, rs, :] = (zgate * jax.nn.sigmoid(zgate)).astype(BF16)


def _inproj(x, ctx, shl, scl, shc, scc, lw, tabs):
    B, L, D = x.shape
    ns = INPROJ_SUB
    nt_lat = L // TOK_TILE
    nt = nt_lat + CTX_LEN // TOK_TILE
    assert nt % ns == 0
    LT = L + CTX_LEN
    rb = TOK_TILE // SUBLANE
    nrb = L // SUBLANE
    blk = ns * TOK_TILE
    full = lambda a: pl.BlockSpec(a.shape, lambda b, t: (0,) * a.ndim)
    tok = lambda w: pl.BlockSpec((1, blk, w), lambda b, t: (b, t, 0))
    tab = pl.BlockSpec((blk, LANE), lambda b, t: (t, 0))
    mod = pl.BlockSpec((1, 1, D), lambda b, t: (b, 0, 0))
    consts = [lw["norm_g"], lw["w_in"], lw["qn_g"], lw["w_uq"], lw["kvn_g"], lw["w_uk"], lw["w_uv_t"], lw["w_gv_t"],
              lw["gq_g"], lw["gk_g"], lw["conv_w"], lw["alog_x"], lw["dtb_x"]]
    x_specs = [pl.BlockSpec((1, TOK_TILE, D), lambda b, t, s=s: (b, jnp.minimum(ns * t + s, nt_lat - 1), 0))
               for s in range(ns)]
    xp_specs = [pl.BlockSpec((1, SUBLANE, D), lambda b, t, s=s: (b, jnp.clip((ns * t + s) * rb - 1, 0, nrb - 1), 0))
                for s in range(ns)]
    xn_specs = [pl.BlockSpec((1, SUBLANE, D), lambda b, t, s=s: (b, jnp.clip((ns * t + s + 1) * rb, 0, nrb - 1), 0))
                for s in range(ns)]
    in_specs = x_specs + xp_specs + xn_specs + [
        pl.BlockSpec((1, CTX_LEN, D), lambda b, t: (b, 0, 0)),
        mod, mod, full(shc), full(scc),
    ] + [full(a) for a in consts] + [tab, tab, tab, tab, full(tabs["gsum"])]
    nh = MLA_HEADS + GQA_HEADS
    head_major = pl.BlockSpec((1, nh, blk, LANE), lambda b, t: (b, 0, t, 0))
    out_w = [(_W_GDN, F32), (_W_BA, F32), (D_MIX, BF16)]
    out_specs = [head_major, head_major, pl.BlockSpec((1, nh * MLA_V, blk), lambda b, t: (b, 0, t))
                 ] + [tok(w) for w, _ in out_w]
    out_shape = [jax.ShapeDtypeStruct((B, nh, LT, LANE), BF16), jax.ShapeDtypeStruct((B, nh, LT, LANE), BF16),
                 jax.ShapeDtypeStruct((B, nh * MLA_V, LT), BF16)
                 ] + [jax.ShapeDtypeStruct((B, LT, w), dt) for w, dt in out_w]
    return pl.pallas_call(
        functools.partial(_inproj_kernel, nt_lat, ns),
        grid=(B, nt // ns),
        in_specs=in_specs,
        out_specs=out_specs,
        out_shape=out_shape,
        compiler_params=_params("parallel", "arbitrary"),
        name="inproj",
    )(*([x] * (3 * ns)), ctx, shl, scl, shc, scc, *consts,
      tabs["ca"], tabs["sa"], tabs["cb"], tabs["sb"], tabs["gsum"])


def _attn_kernel(q_ref, k_ref, vt_ref, o_ref, s0_scr, s1_scr, p0_scr, p1_scr):
    nh = q_ref.shape[1]
    tq = s0_scr.shape[1]
    n_items = nh * (q_ref.shape[2] // tq)
    assert n_items % 2 == 0
    dv = vt_ref.shape[1] // nh
    ones_rows = jnp.ones((2 * SUBLANE, k_ref.shape[2]), BF16)
    s_scr, p_scr = (s0_scr, s1_scr), (p0_scr, p1_scr)

    def scores(i, par):
        tile, h = i // nh, i % nh
        q = q_ref[0, h, pl.ds(pl.multiple_of(tile * tq, tq), tq), :]
        s = _dot_nt(k_ref[0, h], q)
        s_scr[par][...] = s
        return jnp.max(s, axis=0, keepdims=True)

    def probs(par, m):
        p_scr[par][...] = jnp.exp2(s_scr[par][...] - m).astype(BF16)

    def values(i, par):
        tile, h = i // nh, i % nh
        r0 = pl.multiple_of(h * dv, dv)
        lhs = jnp.concatenate([vt_ref[0, pl.ds(r0, dv), :], ones_rows], axis=0)
        r = _dot(lhs, p_scr[par][...])
        o_ref[0, tile, pl.ds(r0, dv), :] = (r[0:dv] / r[dv:dv + 1]).astype(o_ref.dtype)

    def stage(i, par, m_i):
        values(i - 1, 1 - par)
        m_next = scores(i + 1, 1 - par)
        probs(par, m_i)
        return m_next

    m0 = scores(0, 0)
    m1 = scores(1, 1)
    probs(0, m0)

    def body(j, m_odd):
        i = 2 * j + 1
        m_even = stage(i, 1, m_odd)
        return stage(i + 1, 0, m_even)

    m_last = lax.fori_loop(0, (n_items - 2) // 2, body, m1)
    values(n_items - 2, 0)
    probs(1, m_last)
    values(n_items - 1, 1)


def _attention(q, k, v_t, q_start, q_len, k_start, k_len):
    B, nh = q.shape[0], q.shape[1]
    tq = TOK_TILE
    tiles = min(ATTN_TILES_PER_STEP, q_len // tq)
    qb = tiles * tq
    nq = q_len // qb
    q0 = q_start // qb
    kblk = k_start // k_len
    wo = v_t.shape[1]
    return pl.pallas_call(
        _attn_kernel,
        grid=(B, nq),
        in_specs=[pl.BlockSpec((1, nh, qb, LANE), lambda b, i: (b, 0, q0 + i, 0)),
                  pl.BlockSpec((1, nh, k_len, LANE), lambda b, i: (b, 0, kblk, 0)),
                  pl.BlockSpec((1, wo, k_len), lambda b, i: (b, 0, kblk))],
        out_specs=pl.BlockSpec((1, tiles, wo, tq), lambda b, i: (b, i, 0, 0)),
        out_shape=jax.ShapeDtypeStruct((B, q_len // tq, wo, tq), BF16),
        scratch_shapes=[pltpu.VMEM((k_len, tq), F32), pltpu.VMEM((k_len, tq), F32),
                        pltpu.VMEM((k_len, tq), BF16), pltpu.VMEM((k_len, tq), BF16)],
        compiler_params=_params("parallel", "arbitrary"),
        name="attention",
    )(q, k, v_t)


def _gdn_kernel(n_lat, n_all, qkv_ref, bg_ref, o_ref, mp_ref, qp_ref, cc_ref, gl_ref):
    C, W, U = GDN_CHUNK, GDN_WIDTH, GDN_UNROLL
    n_ctx = n_all - n_lat

    ri = lax.broadcasted_iota(jnp.int32, (C, W), 0)
    cj = lax.broadcasted_iota(jnp.int32, (C, W), 1) % C
    lane_head = lax.broadcasted_iota(jnp.int32, (C, W), 1) // GDN_DV
    diag = ri == cj
    eye_lb = jnp.where(diag, 1.0, 0.0).astype(F32)
    blk4 = jnp.where(ri // 4 == cj // 4, 1.0, 0.0).astype(F32)
    lvl_masks = [jnp.where(jnp.logical_and(ri // (2 * bs) == cj // (2 * bs), ri // bs != cj // bs), 1.0, 0.0
                           ).astype(F32) for bs in (4, 8, 16, 32)]
    r2 = lax.broadcasted_iota(jnp.int32, (C, C), 0)
    c2 = lax.broadcasted_iota(jnp.int32, (C, C), 1)
    ones8 = jnp.ones((SUBLANE, C), F32)
    rb = lax.broadcasted_iota(jnp.int32, (GDN_HEADS * C, W), 0) // C
    cbk = lax.broadcasted_iota(jnp.int32, (GDN_HEADS * C, W), 1) // C
    bmask = jnp.where(rb == cbk, 1.0, 0.0).astype(BF16)
    bmask_f = bmask.astype(F32)
    incl = [ri >= cj, ri <= cj]
    incl_t = [jnp.where(cj >= ri, 1.0, 0.0).astype(F32), jnp.where(cj <= ri, 1.0, 0.0).astype(F32)]
    tri = [jnp.where(r2 >= c2, 1.0, 0.0).astype(F32), jnp.where(r2 <= c2, 1.0, 0.0).astype(F32)]
    last_row = [C - 1, 0]

    def bd(y):
        yb = y.astype(BF16)
        return jnp.concatenate([yb] * GDN_HEADS, axis=0) * bmask

    def mm(x, ybd):
        return _dot(x.astype(BF16), ybd)

    def diag_blocks(full):
        acc = full[0:C]
        for h in range(1, GDN_HEADS):
            acc = acc + full[C * h:C * (h + 1)]
        return acc

    def expand(bg, col0):
        out = jnp.broadcast_to(bg[:, col0:col0 + 1], (C, W))
        for h in range(1, GDN_HEADS):
            out = jnp.where(lane_head == h, jnp.broadcast_to(bg[:, col0 + h:col0 + h + 1], (C, W)), out)
        return out

    def phase1(it, carry):
        probs = [(c, d) for c in range(U) for d in range(2)]
        each = lambda f, *ls: [f(*a) for a in zip(*ls)]
        per = lambda f: [f(c, d) for c, d in probs]
        rows = [pl.multiple_of((it * U + c) * C, C) for c in range(U)]
        q = [qkv_ref[0, pl.ds(r, C), 0:W] for r in rows]
        k = [qkv_ref[0, pl.ds(r, C), W:2 * W] for r in rows]
        v = [qkv_ref[0, pl.ds(r, C), 2 * W:3 * W] for r in rows]
        bg = [bg_ref[0, pl.ds(r, C), :] for r in rows]
        kbf = each(lambda x: x.astype(BF16), k)
        gram = each(lambda kb_, q_: _dot_nt(jnp.concatenate([kb_, q_.astype(BF16)], axis=0),
                                            jnp.concatenate([kb_] * GDN_HEADS, axis=0) * bmask), kbf, q)
        beta = per(lambda c, d: expand(bg[c], GDN_HEADS * d))
        g = per(lambda c, d: expand(bg[c], _N_BA + GDN_HEADS * d))
        dirs = [d for _, d in probs]
        chunk = [c for c, _ in probs]
        gc = each(lambda x, d: jnp.dot(tri[d], x, precision=HIGHEST, preferred_element_type=F32), g, dirs)
        g2 = each(lambda x, d: jnp.dot(ones8, x * incl_t[d], precision=HIGHEST, preferred_element_type=F32), g, dirs)
        dlt = each(lambda a, b: a - jnp.broadcast_to(b[0:1], (C, W)), gc, g2)
        dec = each(lambda x, d: jnp.where(incl[d], jnp.exp(jnp.where(incl[d], x, 0.0)), 0.0), dlt, dirs)
        gcl = each(lambda x, d: x[last_row[d]:last_row[d] + 1], gc, dirs)
        egc = each(jnp.exp, gc)
        a_mat = each(lambda c, b, de: gram[c][0:C] * b * jnp.where(diag, 0.0, de), chunk, beta, dec)
        a_intra = each(lambda c, de: (gram[c][C:2 * C] * de).astype(BF16), chunk, dec)

        b_0 = each(lambda a: -(a * blk4), a_mat)
        p_0 = each(lambda b: b + mm(b, bd(b)), b_0)
        t_mat = each(lambda b, p: eye_lb + b + mm(p, bd(b)), b_0, p_0)
        for msk in lvl_masks:
            y = each(lambda a, x: mm(a * msk, bd(x)), a_mat, t_mat)
            t_mat = each(lambda x, y_: x - mm(x, bd(y_)), t_mat, y)

        u = each(lambda t, c, b: mm(t, bd(v[c] * b)), t_mat, chunk, beta)
        w = each(lambda t, c, b, e: mm(t, bd(k[c] * b * e)), t_mat, chunk, beta, egc)
        kd_t = each(lambda c, gl_, gc_: jnp.transpose(k[c] * jnp.exp(gl_ - gc_)).astype(BF16), chunk, gcl, gc)
        kwu = each(lambda kt, w_, u_: _dot(kt, jnp.concatenate([w_, u_], axis=1).astype(BF16)), kd_t, w, u)
        aw = each(lambda a, w_: _dot(a, bd(w_)), a_intra, w)
        au = each(lambda a, u_: _dot(a, bd(u_)), a_intra, u)
        for p, (c, d) in enumerate(probs):
            n = it * U + c
            mp_ref[d, n] = (-diag_blocks(kwu[p][:, 0:W] * bmask_f)).astype(BF16)
            cc_ref[d, n] = diag_blocks(kwu[p][:, W:2 * W] * bmask_f)
            qp_ref[d, n] = (q[c] * egc[p] - aw[p]).astype(BF16)
            gl_ref[d, n] = jnp.broadcast_to(jnp.exp(gcl[p]), (SUBLANE, W))
        for c in range(U):
            o_ref[0, pl.ds(rows[c], C), :] = au[2 * c] + au[2 * c + 1]
        return carry

    assert n_all % U == 0
    lax.fori_loop(0, n_all // U, phase1, 0)

    def phase2(i, carry):
        n = (jnp.where(i < n_ctx, n_lat + i, i - n_ctx), n_all - 1 - i)
        r = [_dot(jnp.concatenate([mp_ref[d, n[d]], qp_ref[d, n[d]]], axis=0), bd(carry[d])) for d in range(2)]
        new = []
        for d in range(2):
            r0 = pl.multiple_of(n[d] * C, C)
            o_ref[0, pl.ds(r0, C), :] += r[d][C:2 * C]
            new.append(gl_ref[d, n[d]][0:1] * carry[d] + r[d][0:C] + cc_ref[d, n[d]])
        return tuple(new)

    zero = jnp.zeros((C, W), F32)
    lax.fori_loop(0, n_all, phase2, (zero, zero))


def _gdn(qkv, bg, n_lat_tokens):
    B, LT, _ = qkv.shape
    n_all = LT // GDN_CHUNK
    n_lat = n_lat_tokens // GDN_CHUNK
    W = GDN_WIDTH
    return pl.pallas_call(
        functools.partial(_gdn_kernel, n_lat, n_all),
        grid=(B,),
        in_specs=[
            pl.BlockSpec((1, LT, 3 * W), lambda b: (b, 0, 0)),
            pl.BlockSpec((1, LT, LANE), lambda b: (b, 0, 0)),
        ],
        out_specs=pl.BlockSpec((1, LT, W), lambda b: (b, 0, 0)),
        out_shape=jax.ShapeDtypeStruct((B, LT, W), F32),
        scratch_shapes=[
            pltpu.VMEM((2, n_all, GDN_CHUNK, W), BF16),
            pltpu.VMEM((2, n_all, GDN_CHUNK, W), BF16),
            pltpu.VMEM((2, n_all, GDN_CHUNK, W), F32),
            pltpu.VMEM((2, n_all, SUBLANE, W), F32),
        ],
        compiler_params=_params("parallel"),
        name="gdn",
    )(qkv, bg)


def _merge_kernel(final, x_ref, oa_ref, og_ref, sg_ref, gt_ref, ong_ref, gsum_ref, wout_ref, fg_ref, o_ref):
    wa = oa_ref.shape[2]
    subs = [slice(TOK_TILE * s, TOK_TILE * (s + 1)) for s in range(oa_ref.shape[1])]
    ogs = [og_ref[0, sl, :] for sl in subs]
    ssq = [_group_sum(og * og, gsum_ref[...]) for og in ogs]
    ms = []
    for s, sl in enumerate(subs):
        ogn = ogs[s] * lax.rsqrt(ssq[s] * (1.0 / GDN_DV) + EPS) * ong_ref[...]
        ma = (jnp.transpose(oa_ref[0, s].astype(F32)) * sg_ref[0, sl, 0:wa].astype(F32)).astype(BF16)
        mg = (ogn * sg_ref[0, sl, wa:].astype(F32)).astype(BF16)
        ms.append((ma, mg))
    ys = [_dot(ma, wout_ref[0:wa, :]) + _dot(mg, wout_ref[wa:, :]) for ma, mg in ms]
    for s, sl in enumerate(subs):
        xn = x_ref[0, sl, :] + gt_ref[0] * ys[s]
        if final:
            xn = xn * lax.rsqrt(jnp.mean(xn * xn, axis=-1, keepdims=True) + EPS) * fg_ref[...]
        o_ref[0, sl, :] = xn


def _merge(x, o_attn, o_gdn, sg, gt, lw, gsum, final_g, row_start, final):
    B, L, D = x.shape
    sub = min(MERGE_SUB, L // TOK_TILE)
    rows = sub * TOK_TILE
    nt = L // rows
    t0 = row_start // rows
    full = lambda a: pl.BlockSpec(a.shape, lambda b, t: (0,) * a.ndim)
    return pl.pallas_call(
        functools.partial(_merge_kernel, final),
        grid=(B, nt),
        in_specs=[
            pl.BlockSpec((1, rows, D), lambda b, t: (b, t, 0)),
            pl.BlockSpec((1, sub, o_attn.shape[2], TOK_TILE), lambda b, t: (b, t, 0, 0)),
            pl.BlockSpec((1, rows, GDN_WIDTH), lambda b, t: (b, t0 + t, 0)),
            pl.BlockSpec((1, rows, D_MIX), lambda b, t: (b, t0 + t, 0)),
            pl.BlockSpec((1, 1, D), lambda b, t: (b, 0, 0)),
            full(lw["on_g"]), full(gsum), full(lw["w_out"]), full(final_g),
        ],
        out_specs=pl.BlockSpec((1, rows, D), lambda b, t: (b, t, 0)),
        out_shape=jax.ShapeDtypeStruct((B, L, D), F32),
        compiler_params=_params("parallel", "arbitrary"),
        name="merge",
    )(x, o_attn, o_gdn, sg, gt, lw["on_g"], gsum, lw["w_out"], final_g)


def _layer_weights(l, w_in, norm_g, mla_qn_g, mla_w_uq, mla_kvn_g, mla_w_ukv, gqa_qn_g, gqa_kn_g,
                   gdn_conv_w, gdn_a_log, gdn_dt_bias, gdn_on_g, w_out):
    D = D_MODEL
    w = w_in[l]
    splits = np.cumsum([0, MLA_Q_RANK, MLA_KV_RANK, MLA_ROPE, GQA_HEADS * GQA_DIM, GQA_KV_HEADS * GQA_DIM,
                        GQA_KV_HEADS * GQA_DIM, 3 * GDN_HEADS * GDN_DK, 2 * GDN_HEADS, 2 * GDN_HEADS, D_MIX])
    seg = [w[:, int(a):int(b)] for a, b in zip(splits[:-1], splits[1:])]
    cq, ckv, kr, gq, gk, gv, gdn, bb, aa, gate = seg
    z = lambda n: jnp.zeros((D, n), w.dtype)
    kr_p = jnp.concatenate([z(MLA_NOPE), kr, z(LANE - MLA_NOPE - MLA_ROPE)], axis=1)
    k0, k1 = gk[:, :GQA_DIM], gk[:, GQA_DIM:]
    gk2 = jnp.concatenate([k0, k1, k1, k0], axis=1)
    ba = jnp.concatenate([bb, aa, z(LANE - 2 * _N_BA)], axis=1)
    w_p = jnp.concatenate([cq, ckv, kr_p, gq, gk2, gdn, ba, gate], axis=1).astype(BF16)

    dqk = MLA_NOPE + MLA_ROPE
    uq = mla_w_uq[l].reshape(MLA_Q_RANK, MLA_HEADS, dqk)
    uq_p = jnp.concatenate([uq, jnp.zeros((MLA_Q_RANK, MLA_HEADS, LANE - dqk), uq.dtype)], axis=2)
    ukv = mla_w_ukv[l].reshape(MLA_KV_RANK, MLA_HEADS, MLA_NOPE + MLA_V)
    uk_p = jnp.concatenate([ukv[:, :, :MLA_NOPE],
                            jnp.zeros((MLA_KV_RANK, MLA_HEADS, LANE - MLA_NOPE), ukv.dtype)], axis=2)
    uv = ukv[:, :, MLA_NOPE:]
    return {
        "norm_g": norm_g[l].reshape(1, D),
        "w_in": w_p,
        "qn_g": mla_qn_g[l].reshape(1, -1),
        "w_uq": uq_p.reshape(MLA_Q_RANK, MLA_HEADS * LANE).astype(BF16),
        "kvn_g": mla_kvn_g[l].reshape(1, -1),
        "w_uk": uk_p.reshape(MLA_KV_RANK, MLA_HEADS * LANE).astype(BF16),
        "w_uv_t": uv.reshape(MLA_KV_RANK, MLA_HEADS * MLA_V).T.astype(BF16),
        "w_gv_t": gv.T.astype(BF16),
        "gq_g": (jnp.tile(gqa_qn_g[l], GQA_HEADS) * (GQA_SCALE * LOG2E)).reshape(1, -1),
        "gk_g": jnp.tile(gqa_kn_g[l], 2 * 2).reshape(1, -1),
        "conv_w": jnp.concatenate([gdn_conv_w[l], jnp.zeros((SUBLANE - GDN_CONV, _W_GDN), F32)], axis=0),
        "alog_x": jnp.concatenate([jnp.zeros((_N_BA,), F32), gdn_a_log[l].reshape(-1),
                                   jnp.zeros((LANE - 2 * _N_BA,), F32)]).reshape(1, LANE),
        "dtb_x": jnp.concatenate([jnp.zeros((_N_BA,), F32), gdn_dt_bias[l].reshape(-1),
                                  jnp.zeros((LANE - 2 * _N_BA,), F32)]).reshape(1, LANE),
        "on_g": jnp.tile(gdn_on_g[l], GDN_HEADS).reshape(1, -1),
        "w_out": w_out[l].astype(BF16),
    }


def _rope_tables(seq_len):
    LT = seq_len + CTX_LEN
    t = np.arange(seq_len)
    rows, cols = (t // GRID_W).astype(np.float64), (t % GRID_W).astype(np.float64)

    def fill(c, s, lane0, d, pos):
        half = d // 2
        inv = ROPE_BASE ** (-np.arange(0, d, 2, dtype=np.float64) / d)
        ang = pos[:, None] * inv[None, :]
        c[:seq_len, lane0:lane0 + half] = np.cos(ang)
        c[:seq_len, lane0 + half:lane0 + d] = np.cos(ang)
        s[:seq_len, lane0:lane0 + half] = -np.sin(ang)
        s[:seq_len, lane0 + half:lane0 + d] = np.sin(ang)

    ca, sa = np.ones((LT, LANE)), np.zeros((LT, LANE))
    fill(ca, sa, MLA_NOPE, MLA_ROPE // 2, rows)
    fill(ca, sa, MLA_NOPE + MLA_ROPE // 2, MLA_ROPE // 2, cols)
    cb, sb = np.ones((LT, LANE)), np.zeros((LT, LANE))
    for h0 in (0, GQA_DIM):
        fill(cb, sb, h0, GQA_DIM // 2, rows)
        fill(cb, sb, h0 + GQA_DIM // 2, GQA_DIM // 2, cols)
    lane = np.arange(3 * LANE)
    gsum = (lane[:, None] // GQA_DIM == lane[None, :] // GQA_DIM).astype(np.float32)
    return {"ca": jnp.asarray(ca, F32), "sa": jnp.asarray(sa, F32),
            "cb": jnp.asarray(cb, F32), "sb": jnp.asarray(sb, F32),
            "gsum": jnp.asarray(gsum, BF16)}


def kernel(x, c, ctx, c_ctx, w_ada, b_ada, norm_g, w_in, mla_qn_g, mla_w_uq, mla_kvn_g, mla_w_ukv,
           gqa_qn_g, gqa_kn_g, gdn_conv_w, gdn_a_log, gdn_dt_bias, gdn_on_g, w_out, final_g):
    B, L, D = x.shape
    assert D == D_MODEL and L % TOK_TILE == 0 and ctx.shape[1] == CTX_LEN
    tabs = _rope_tables(L)
    rows = -(-(B + 1) // SUBLANE) * SUBLANE
    cvec = jnp.concatenate([c, c_ctx[None, :], jnp.zeros((rows - B - 1, D), F32)], axis=0)
    mod = _adaln(cvec, w_ada, b_ada)
    fg = final_g.reshape(1, D)
    gs256 = tabs["gsum"][:GDN_WIDTH, :GDN_WIDTH]

    for l in range(DEPTH):
        lw = _layer_weights(l, w_in, norm_g, mla_qn_g, mla_w_uq, mla_kvn_g, mla_w_ukv, gqa_qn_g,
                            gqa_kn_g, gdn_conv_w, gdn_a_log, gdn_dt_bias, gdn_on_g, w_out)
        ml = mod[l, :B].reshape(B, 1, 3 * D)
        mc = mod[l, B].reshape(1, 3 * D)
        shl, scl, gtl = ml[..., :D], ml[..., D:2 * D], ml[..., 2 * D:]
        shc, scc, gtc = mc[:, :D], mc[:, D:2 * D], mc[:, 2 * D:]
        q, k, v_t, qkv, bg, sg = _inproj(x, ctx, shl, scl, shc, scc, lw, tabs)
        o_gdn = _gdn(qkv, bg, L)
        o_lat = _attention(q, k, v_t, 0, L, 0, L + CTX_LEN)
        last = l == DEPTH - 1
        x_new = _merge(x, o_lat, o_gdn, sg, gtl, lw, gs256, fg, 0, last)
        if not last:
            o_ctx = _attention(q, k, v_t, L, CTX_LEN, L, CTX_LEN)
            gtc_b = jnp.broadcast_to(gtc.reshape(1, 1, D), (B, 1, D))
            ctx = _merge(ctx, o_ctx, o_gdn, sg, gtc_b, lw, gs256, fg, L, False)
        x = x_new
    return x
```

```python
import functools
import math

import numpy as np
import jax
import jax.numpy as jnp
from jax import lax
from jax.experimental import pallas as pl
from jax.experimental.pallas import tpu as pltpu

F32 = jnp.float32
BF16 = jnp.bfloat16
HIGHEST = lax.Precision.HIGHEST

D_MODEL = 1024
DEPTH = 2
CTX_LEN = 256
GRID_W = 64
D_MIX = 1024
EPS = 1e-6
ROPE_BASE = 10000.0

MLA_HEADS = 6
MLA_NOPE = 64
MLA_ROPE = 32
MLA_V = 64
MLA_Q_RANK = 384
MLA_KV_RANK = 256
LOG2E = math.log2(math.e)
MLA_SCALE = (MLA_NOPE + MLA_ROPE) ** -0.5

GQA_HEADS = 6
GQA_KV_HEADS = 2
GQA_DIM = 64
GQA_SCALE = GQA_DIM ** -0.5

GDN_HEADS = 4
GDN_DK = 64
GDN_DV = 64
GDN_WIDTH = GDN_HEADS * GDN_DV
GDN_CONV = 5
GDN_CHUNK = 64

LANE = 128
SUBLANE = 8
TOK_TILE = 256
ATTN_TILES_PER_STEP = 4
INPROJ_SUB = 3
MERGE_SUB = 4
GDN_UNROLL = 6
VMEM_LIMIT = 48 * 1024 * 1024
GDN_VMEM_LIMIT = 58 * 1024 * 1024

_W_CQ = MLA_Q_RANK
_W_CKV = MLA_KV_RANK
_W_KR = LANE
_W_GQ = GQA_HEADS * GQA_DIM
_W_GK2 = 2 * LANE
_W_GDN = 3 * GDN_WIDTH
_W_BA = LANE
_N_BA = 2 * GDN_HEADS
_W_GATE = D_MIX
_OFFS = np.cumsum([0, _W_CQ, _W_CKV, _W_KR, _W_GQ, _W_GK2, _W_GDN, _W_BA, _W_GATE])
(_O_CQ, _O_CKV, _O_KR, _O_GQ, _O_GK2, _O_GDN, _O_BA, _O_GATE, D_IN_P) = [int(v) for v in _OFFS]


def _dot(a, b):
    return jnp.dot(a, b, preferred_element_type=F32)


def _dot_nt(a, b):
    return lax.dot_general(a, b, (((1,), (1,)), ((), ())), preferred_element_type=F32)


def _params(*sem):
    return pltpu.CompilerParams(dimension_semantics=sem, vmem_limit_bytes=VMEM_LIMIT)


def _adaln_kernel(c_ref, w_ref, b_ref, o_ref):
    c = c_ref[...]
    s = c * jax.nn.sigmoid(c)
    o_ref[0] = jnp.dot(s, w_ref[0], precision=HIGHEST, preferred_element_type=F32) + b_ref[0]


def _adaln(cvec, w_ada, b_ada):
    rows = cvec.shape[0]
    nblk = w_ada.shape[2] // D_MODEL
    return pl.pallas_call(
        _adaln_kernel,
        grid=(DEPTH, nblk),
        in_specs=[
            pl.BlockSpec((rows, D_MODEL), lambda l, j: (0, 0)),
            pl.BlockSpec((1, D_MODEL, D_MODEL), lambda l, j: (l, 0, j)),
            pl.BlockSpec((1, 1, D_MODEL), lambda l, j: (l, 0, j)),
        ],
        out_specs=pl.BlockSpec((1, rows, D_MODEL), lambda l, j: (l, 0, j)),
        out_shape=jax.ShapeDtypeStruct((DEPTH, rows, 3 * D_MODEL), F32),
        compiler_params=_params("arbitrary", "arbitrary"),
        name="adaln",
    )(cvec, w_ada, b_ada.reshape(DEPTH, 1, 3 * D_MODEL))


def _group_sum(sq, gsum):
    return _dot(sq.astype(BF16), gsum)


def _rope_block(xb, c, s, half):
    lane = lax.broadcasted_iota(jnp.int32, xb.shape, 1)
    up = pltpu.roll(xb, LANE - half, 1)
    dn = pltpu.roll(xb, half, 1)
    partner = jnp.where((lane % (2 * half)) < half, up, dn)
    return xb * c + partner * s


def _inproj_kernel(nt_lat, ns, *refs):
    x_refs, xp_refs, xn_refs = refs[0:ns], refs[ns:2 * ns], refs[2 * ns:3 * ns]
    (ctx_ref, shl_ref, scl_ref, shc_ref, scc_ref, ng_ref, win_ref, qng_ref, wuq_ref, kvng_ref, wuk_ref,
     wuvt_ref, wgvt_ref, gqg_ref, gkg_ref, cw_ref, alog_ref, dtb_ref, ca_ref, sa_ref, cb_ref, sb_ref,
     gsum_ref, q_ref, k_ref, vt_ref, qkv_ref, bg_ref, sg_ref) = refs[3 * ns:]
    t = pl.program_id(1)
    tm = TOK_TILE
    nv = MLA_HEADS * MLA_V
    gsum = gsum_ref[...]
    lane = lax.broadcasted_iota(jnp.int32, (tm, LANE), 1)
    row = lax.broadcasted_iota(jnp.int32, (tm + 2 * SUBLANE, 1), 0)
    tiles = [ns * t + s for s in range(ns)]
    rows = [slice(tm * s, tm * (s + 1)) for s in range(ns)]

    hbe, hb = [], []
    for s in range(ns):
        is_ctx = tiles[s] == nt_lat
        xt = jnp.where(is_ctx, ctx_ref[0], x_refs[s][0])
        xe = jnp.concatenate([xp_refs[s][0], xt, xn_refs[s][0]], axis=0)
        sh = jnp.where(is_ctx, shc_ref[...], shl_ref[0])
        sc = jnp.where(is_ctx, scc_ref[...], scl_ref[0])
        ms = jnp.mean(xe * xe, axis=-1, keepdims=True)
        he = (xe * lax.rsqrt(ms + EPS) * ng_ref[...]) * (1.0 + sc) + sh
        hbe.append(he.astype(BF16))
        hb.append(he[SUBLANE:SUBLANE + tm].astype(BF16))

    z = []
    for s in range(ns):
        zs = dict(
            g=_dot(hbe[s], win_ref[:, _O_GDN:_O_GDN + _W_GDN]),
            cq=_dot(hb[s], win_ref[:, _O_CQ:_O_CQ + _W_CQ]),
            ckv=_dot(hb[s], win_ref[:, _O_CKV:_O_CKV + _W_CKV]),
            q=_dot(hb[s], win_ref[:, _O_GQ:_O_GQ + _W_GQ]),
            k=_dot(hb[s], win_ref[:, _O_GK2:_O_GK2 + _W_GK2]),
            kr=_dot(hb[s], win_ref[:, _O_KR:_O_KR + _W_KR]),
            ba=_dot(hb[s], win_ref[:, _O_BA:_O_BA + _W_BA]))
        vg = _dot_nt(wgvt_ref[...], hb[s]).astype(BF16)
        for j in range(GQA_HEADS):
            kv = j // (GQA_HEADS // GQA_KV_HEADS)
            vt_ref[0, nv + GQA_DIM * j:nv + GQA_DIM * (j + 1), rows[s]] = vg[GQA_DIM * kv:GQA_DIM * (kv + 1)]
        zs["gate"] = _dot(hb[s], win_ref[:, _O_GATE:_O_GATE + _W_GATE])
        z.append(zs)

    for s in range(ns):
        zs, rs = z[s], rows[s]
        is_ctx = tiles[s] == nt_lat
        ca, sa = ca_ref[rs, :], sa_ref[rs, :]
        cb, sb = cb_ref[rs, :], sb_ref[rs, :]

        has_prev = jnp.logical_and(tiles[s] > 0, jnp.logical_not(is_ctx))
        has_next = tiles[s] < nt_lat - 1
        keep = jnp.logical_and(jnp.logical_or(row >= SUBLANE, has_prev),
                               jnp.logical_or(row < SUBLANE + tm, has_next))
        zg = jnp.where(keep, zs["g"], 0.0)
        pad = GDN_CONV // 2
        y = jnp.zeros((tm, _W_GDN), F32)
        for j in range(GDN_CONV):
            o = SUBLANE - pad + j
            y = y + zg[o:o + tm] * cw_ref[j:j + 1, :]
        y = y * jax.nn.sigmoid(y)
        gs = gsum[:GDN_WIDTH, :GDN_WIDTH]
        qg = y[:, :GDN_WIDTH]
        kg = y[:, GDN_WIDTH:2 * GDN_WIDTH]
        qkv_ref[0, rs, 0:GDN_WIDTH] = qg * lax.rsqrt(_group_sum(qg * qg, gs) + EPS) * (GDN_DK ** -0.5)
        qkv_ref[0, rs, GDN_WIDTH:2 * GDN_WIDTH] = kg * lax.rsqrt(_group_sum(kg * kg, gs) + EPS)
        qkv_ref[0, rs, 2 * GDN_WIDTH:] = y[:, 2 * GDN_WIDTH:]
        za = zs["ba"] + dtb_ref[...]
        softplus = jnp.maximum(za, 0.0) + jnp.log(1.0 + jnp.exp(-jnp.abs(za)))
        bg_ref[0, rs, :] = jnp.where(lane < _N_BA, jax.nn.sigmoid(zs["ba"]), -jnp.exp(alog_ref[...]) * softplus)

        zcq = zs["cq"]
        cqn = zcq * lax.rsqrt(jnp.mean(zcq * zcq, axis=-1, keepdims=True) + EPS) * qng_ref[...]
        q = _dot(cqn.astype(BF16), wuq_ref[...]) * (MLA_SCALE * LOG2E)
        for h in range(MLA_HEADS):
            sl = slice(LANE * h, LANE * (h + 1))
            q_ref[0, h, rs, :] = _rope_block(q[:, sl], ca, sa, MLA_ROPE // 4).astype(BF16)

        zckv = zs["ckv"]
        ckvn = (zckv * lax.rsqrt(jnp.mean(zckv * zckv, axis=-1, keepdims=True) + EPS) * kvng_ref[...]).astype(BF16)
        kr = _rope_block(zs["kr"], ca, sa, MLA_ROPE // 4)
        kn = _dot(ckvn, wuk_ref[...])
        for h in range(MLA_HEADS):
            sl = slice(LANE * h, LANE * (h + 1))
            k_ref[0, h, rs, :] = (kn[:, sl] + kr).astype(BF16)
        vt_ref[0, 0:nv, rs] = _dot_nt(wuvt_ref[...], ckvn).astype(BF16)

        zq, zk = zs["q"], zs["k"]
        qn = zq * lax.rsqrt(_group_sum(zq * zq, gsum) * (1.0 / GQA_DIM) + EPS) * gqg_ref[...]
        for p in range(GQA_HEADS // 2):
            blk = _rope_block(qn[:, LANE * p:LANE * (p + 1)], cb, sb, GQA_DIM // 4)
            q_ref[0, MLA_HEADS + 2 * p, rs, :] = jnp.where(lane < GQA_DIM, blk, 0.0).astype(BF16)
            q_ref[0, MLA_HEADS + 2 * p + 1, rs, :] = jnp.where(lane >= GQA_DIM, blk, 0.0).astype(BF16)
        kn2 = zk * lax.rsqrt(_group_sum(zk * zk, gsum[:_W_GK2, :_W_GK2]) * (1.0 / GQA_DIM) + EPS) * gkg_ref[...]
        kblks = [_rope_block(kn2[:, LANE * p:LANE * (p + 1)], cb, sb, GQA_DIM // 4).astype(BF16) for p in range(2)]
        for j in range(GQA_HEADS):
            kv = j // (GQA_HEADS // GQA_KV_HEADS)
            k_ref[0, MLA_HEADS + j, rs, :] = kblks[0 if (j % 2) == kv else 1]

        zgate = zs["gate"]
        sg_ref[0, rs, :] = (zgate * jax.nn.sigmoid(zgate)).astype(BF16)


def _inproj(x, ctx, shl, scl, shc, scc, lw, tabs):
    B, L, D = x.shape
    ns = INPROJ_SUB
    nt_lat = L // TOK_TILE
    nt = nt_lat + CTX_LEN // TOK_TILE
    assert nt % ns == 0
    LT = L + CTX_LEN
    rb = TOK_TILE // SUBLANE
    nrb = L // SUBLANE
    blk = ns * TOK_TILE
    full = lambda a: pl.BlockSpec(a.shape, lambda b, t: (0,) * a.ndim)
    tok = lambda w: pl.BlockSpec((1, blk, w), lambda b, t: (b, t, 0))
    tab = pl.BlockSpec((blk, LANE), lambda b, t: (t, 0))
    mod = pl.BlockSpec((1, 1, D), lambda b, t: (b, 0, 0))
    consts = [lw["norm_g"], lw["w_in"], lw["qn_g"], lw["w_uq"], lw["kvn_g"], lw["w_uk"], lw["w_uv_t"], lw["w_gv_t"],
              lw["gq_g"], lw["gk_g"], lw["conv_w"], lw["alog_x"], lw["dtb_x"]]
    x_specs = [pl.BlockSpec((1, TOK_TILE, D), lambda b, t, s=s: (b, jnp.minimum(ns * t + s, nt_lat - 1), 0))
               for s in range(ns)]
    xp_specs = [pl.BlockSpec((1, SUBLANE, D), lambda b, t, s=s: (b, jnp.clip((ns * t + s) * rb - 1, 0, nrb - 1), 0))
                for s in range(ns)]
    xn_specs = [pl.BlockSpec((1, SUBLANE, D), lambda b, t, s=s: (b, jnp.clip((ns * t + s + 1) * rb, 0, nrb - 1), 0))
                for s in range(ns)]
    in_specs = x_specs + xp_specs + xn_specs + [
        pl.BlockSpec((1, CTX_LEN, D), lambda b, t: (b, 0, 0)),
        mod, mod, full(shc), full(scc),
    ] + [full(a) for a in consts] + [tab, tab, tab, tab, full(tabs["gsum"])]
    nh = MLA_HEADS + GQA_HEADS
    head_major = pl.BlockSpec((1, nh, blk, LANE), lambda b, t: (b, 0, t, 0))
    out_w = [(_W_GDN, F32), (_W_BA, F32), (D_MIX, BF16)]
    out_specs = [head_major, head_major, pl.BlockSpec((1, nh * MLA_V, blk), lambda b, t: (b, 0, t))
                 ] + [tok(w) for w, _ in out_w]
    out_shape = [jax.ShapeDtypeStruct((B, nh, LT, LANE), BF16), jax.ShapeDtypeStruct((B, nh, LT, LANE), BF16),
                 jax.ShapeDtypeStruct((B, nh * MLA_V, LT), BF16)
                 ] + [jax.ShapeDtypeStruct((B, LT, w), dt) for w, dt in out_w]
    return pl.pallas_call(
        functools.partial(_inproj_kernel, nt_lat, ns),
        grid=(B, nt // ns),
        in_specs=in_specs,
        out_specs=out_specs,
        out_shape=out_shape,
        compiler_params=_params("parallel", "arbitrary"),
        name="inproj",
    )(*([x] * (3 * ns)), ctx, shl, scl, shc, scc, *consts,
      tabs["ca"], tabs["sa"], tabs["cb"], tabs["sb"], tabs["gsum"])


def _attn_kernel(q_ref, k_ref, vt_ref, o_ref, s0_scr, s1_scr, p0_scr, p1_scr):
    nh = q_ref.shape[1]
    tq = s0_scr.shape[1]
    n_items = nh * (q_ref.shape[2] // tq)
    assert n_items % 2 == 0
    dv = vt_ref.shape[1] // nh
    ones_rows = jnp.ones((2 * SUBLANE, k_ref.shape[2]), BF16)
    s_scr, p_scr = (s0_scr, s1_scr), (p0_scr, p1_scr)

    def scores(i, par):
        tile, h = i // nh, i % nh
        q = q_ref[0, h, pl.ds(pl.multiple_of(tile * tq, tq), tq), :]
        s = _dot_nt(k_ref[0, h], q)
        s_scr[par][...] = s
        return jnp.max(s, axis=0, keepdims=True)

    def probs(par, m):
        p_scr[par][...] = jnp.exp2(s_scr[par][...] - m).astype(BF16)

    def values(i, par):
        tile, h = i // nh, i % nh
        r0 = pl.multiple_of(h * dv, dv)
        lhs = jnp.concatenate([vt_ref[0, pl.ds(r0, dv), :], ones_rows], axis=0)
        r = _dot(lhs, p_scr[par][...])
        o_ref[0, tile, pl.ds(r0, dv), :] = (r[0:dv] / r[dv:dv + 1]).astype(o_ref.dtype)

    def stage(i, par, m_i):
        values(i - 1, 1 - par)
        m_next = scores(i + 1, 1 - par)
        probs(par, m_i)
        return m_next

    m0 = scores(0, 0)
    m1 = scores(1, 1)
    probs(0, m0)

    def body(j, m_odd):
        i = 2 * j + 1
        m_even = stage(i, 1, m_odd)
        return stage(i + 1, 0, m_even)

    m_last = lax.fori_loop(0, (n_items - 2) // 2, body, m1)
    values(n_items - 2, 0)
    probs(1, m_last)
    values(n_items - 1, 1)


def _attention(q, k, v_t, q_start, q_len, k_start, k_len):
    B, nh = q.shape[0], q.shape[1]
    tq = TOK_TILE
    tiles = min(ATTN_TILES_PER_STEP, q_len // tq)
    qb = tiles * tq
    nq = q_len // qb
    q0 = q_start // qb
    kblk = k_start // k_len
    wo = v_t.shape[1]
    return pl.pallas_call(
        _attn_kernel,
        grid=(B, nq),
        in_specs=[pl.BlockSpec((1, nh, qb, LANE), lambda b, i: (b, 0, q0 + i, 0)),
                  pl.BlockSpec((1, nh, k_len, LANE), lambda b, i: (b, 0, kblk, 0)),
                  pl.BlockSpec((1, wo, k_len), lambda b, i: (b, 0, kblk))],
        out_specs=pl.BlockSpec((1, tiles, wo, tq), lambda b, i: (b, i, 0, 0)),
        out_shape=jax.ShapeDtypeStruct((B, q_len // tq, wo, tq), BF16),
        scratch_shapes=[pltpu.VMEM((k_len, tq), F32), pltpu.VMEM((k_len, tq), F32),
                        pltpu.VMEM((k_len, tq), BF16), pltpu.VMEM((k_len, tq), BF16)],
        compiler_params=_params("parallel", "arbitrary"),
        name="attention",
    )(q, k, v_t)


def _gdn_kernel(n_lat, n_all, nb, qkv_ref, bg_ref, o_ref, *scr):
    C, W, U = GDN_CHUNK, GDN_WIDTH, GDN_UNROLL
    n_ctx = n_all - n_lat
    sets = [scr[0:5], scr[5:10]]
    step = pl.program_id(0)

    ri = lax.broadcasted_iota(jnp.int32, (C, W), 0)
    cj = lax.broadcasted_iota(jnp.int32, (C, W), 1) % C
    lane_head = lax.broadcasted_iota(jnp.int32, (C, W), 1) // GDN_DV
    diag = ri == cj
    eye_lb = jnp.where(diag, 1.0, 0.0).astype(F32)
    blk4 = jnp.where(ri // 4 == cj // 4, 1.0, 0.0).astype(F32)
    lvl_masks = [jnp.where(jnp.logical_and(ri // (2 * bs) == cj // (2 * bs), ri // bs != cj // bs), 1.0, 0.0
                           ).astype(F32) for bs in (4, 8, 16, 32)]
    r2 = lax.broadcasted_iota(jnp.int32, (C, C), 0)
    c2 = lax.broadcasted_iota(jnp.int32, (C, C), 1)
    ones8 = jnp.ones((SUBLANE, C), F32)
    rb = lax.broadcasted_iota(jnp.int32, (GDN_HEADS * C, W), 0) // C
    cbk = lax.broadcasted_iota(jnp.int32, (GDN_HEADS * C, W), 1) // C
    bmask = jnp.where(rb == cbk, 1.0, 0.0).astype(BF16)
    bmask_f = bmask.astype(F32)
    incl = [ri >= cj, ri <= cj]
    incl_t = [jnp.where(cj >= ri, 1.0, 0.0).astype(F32), jnp.where(cj <= ri, 1.0, 0.0).astype(F32)]
    tri = [jnp.where(r2 >= c2, 1.0, 0.0).astype(F32), jnp.where(r2 <= c2, 1.0, 0.0).astype(F32)]
    last_row = [C - 1, 0]

    def bd(y):
        yb = y.astype(BF16)
        return jnp.concatenate([yb] * GDN_HEADS, axis=0) * bmask

    def mm(x, ybd):
        return _dot(x.astype(BF16), ybd)

    def diag_blocks(full):
        acc = full[0:C]
        for h in range(1, GDN_HEADS):
            acc = acc + full[C * h:C * (h + 1)]
        return acc

    def expand(bg, col0):
        out = jnp.broadcast_to(bg[:, col0:col0 + 1], (C, W))
        for h in range(1, GDN_HEADS):
            out = jnp.where(lane_head == h, jnp.broadcast_to(bg[:, col0 + h:col0 + h + 1], (C, W)), out)
        return out

    def phase1(it, wr, tick=lambda: None):
        mp_ref, qp_ref, cc_ref, gl_ref, au_ref = wr
        probs = [(c, d) for c in range(U) for d in range(2)]
        each = lambda f, *ls: [f(*a) for a in zip(*ls)]
        per = lambda f: [f(c, d) for c, d in probs]
        rows = [pl.multiple_of((it * U + c) * C, C) for c in range(U)]
        q = [qkv_ref[0, pl.ds(r, C), 0:W] for r in rows]
        k = [qkv_ref[0, pl.ds(r, C), W:2 * W] for r in rows]
        v = [qkv_ref[0, pl.ds(r, C), 2 * W:3 * W] for r in rows]
        bg = [bg_ref[0, pl.ds(r, C), :] for r in rows]
        kbf = each(lambda x: x.astype(BF16), k)
        gram = each(lambda kb_, q_: _dot_nt(jnp.concatenate([kb_, q_.astype(BF16)], axis=0),
                                            jnp.concatenate([kb_] * GDN_HEADS, axis=0) * bmask), kbf, q)
        tick()
        beta = per(lambda c, d: expand(bg[c], GDN_HEADS * d))
        g = per(lambda c, d: expand(bg[c], _N_BA + GDN_HEADS * d))
        dirs = [d for _, d in probs]
        chunk = [c for c, _ in probs]
        gc = each(lambda x, d: jnp.dot(tri[d], x, precision=HIGHEST, preferred_element_type=F32), g, dirs)
        g2 = each(lambda x, d: jnp.dot(ones8, x * incl_t[d], precision=HIGHEST, preferred_element_type=F32), g, dirs)
        dlt = each(lambda a, b: a - jnp.broadcast_to(b[0:1], (C, W)), gc, g2)
        dec = each(lambda x, d: jnp.where(incl[d], jnp.exp(jnp.where(incl[d], x, 0.0)), 0.0), dlt, dirs)
        gcl = each(lambda x, d: x[last_row[d]:last_row[d] + 1], gc, dirs)
        egc = each(jnp.exp, gc)
        a_mat = each(lambda c, b, de: gram[c][0:C] * b * jnp.where(diag, 0.0, de), chunk, beta, dec)
        a_intra = each(lambda c, de: (gram[c][C:2 * C] * de).astype(BF16), chunk, dec)

        b_0 = each(lambda a: -(a * blk4), a_mat)
        p_0 = each(lambda b: b + mm(b, bd(b)), b_0)
        t_mat = each(lambda b, p: eye_lb + b + mm(p, bd(b)), b_0, p_0)
        tick()
        for msk in lvl_masks:
            y = each(lambda a, x: mm(a * msk, bd(x)), a_mat, t_mat)
            t_mat = each(lambda x, y_: x - mm(x, bd(y_)), t_mat, y)
            tick()

        u = each(lambda t, c, b: mm(t, bd(v[c] * b)), t_mat, chunk, beta)
        w = each(lambda t, c, b, e: mm(t, bd(k[c] * b * e)), t_mat, chunk, beta, egc)
        kd_t = each(lambda c, gl_, gc_: jnp.transpose(k[c] * jnp.exp(gl_ - gc_)).astype(BF16), chunk, gcl, gc)
        kwu = each(lambda kt, w_, u_: _dot(kt, jnp.concatenate([w_, u_], axis=1).astype(BF16)), kd_t, w, u)
        aw = each(lambda a, w_: _dot(a, bd(w_)), a_intra, w)
        au = each(lambda a, u_: _dot(a, bd(u_)), a_intra, u)
        for p, (c, d) in enumerate(probs):
            n = it * U + c
            mp_ref[d, n] = (-diag_blocks(kwu[p][:, 0:W] * bmask_f)).astype(BF16)
            cc_ref[d, n] = diag_blocks(kwu[p][:, W:2 * W] * bmask_f)
            qp_ref[d, n] = (q[c] * egc[p] - aw[p]).astype(BF16)
            gl_ref[d, n] = jnp.broadcast_to(jnp.exp(gcl[p]), (SUBLANE, W))
        for c in range(U):
            au_ref[pl.ds(rows[c], C), :] = au[2 * c] + au[2 * c + 1]

    def phase2(i, carry, rd):
        mp_ref, qp_ref, cc_ref, gl_ref, _ = rd
        n = (jnp.where(i < n_ctx, n_lat + i, i - n_ctx), n_all - 1 - i)
        r = [_dot(jnp.concatenate([mp_ref[d, n[d]], qp_ref[d, n[d]]], axis=0), bd(carry[d])) for d in range(2)]
        new = []
        for d in range(2):
            r0 = pl.multiple_of(n[d] * C, C)
            o_ref[0, pl.ds(r0, C), :] += r[d][C:2 * C]
            new.append(gl_ref[d, n[d]][0:1] * carry[d] + r[d][0:C] + cc_ref[d, n[d]])
        return tuple(new)

    assert n_all % U == 0
    zero = jnp.zeros((C, W), F32)

    @pl.when(step == 0)
    def _():
        for ref in sets[1]:
            ref[...] = jnp.zeros(ref.shape, ref.dtype)

    def fused(p):
        wr, rd = sets[p], sets[1 - p]
        o_ref[0] = rd[4][...]

        def body(it, carry):
            box, left = [carry], list(range(U))

            def tick():
                if left:
                    box[0] = phase2(it * U + left.pop(0), box[0], rd)

            phase1(it, wr, tick)
            while left:
                tick()
            return box[0]

        lax.fori_loop(0, n_all // U, body, (zero, zero))

    for p in range(2):
        pl.when(jnp.logical_and(step < nb, step % 2 == p))(functools.partial(fused, p))

    @pl.when(step == nb)
    def _():
        rd = sets[(nb - 1) % 2]
        o_ref[0] = rd[4][...]
        lax.fori_loop(0, n_all, lambda i, c: phase2(i, c, rd), (zero, zero))


def _gdn(qkv, bg, n_lat_tokens):
    B, LT, _ = qkv.shape
    n_all = LT // GDN_CHUNK
    n_lat = n_lat_tokens // GDN_CHUNK
    W = GDN_WIDTH
    one_set = [
        pltpu.VMEM((2, n_all, GDN_CHUNK, W), BF16),
        pltpu.VMEM((2, n_all, GDN_CHUNK, W), BF16),
        pltpu.VMEM((2, n_all, GDN_CHUNK, W), F32),
        pltpu.VMEM((2, n_all, SUBLANE, W), F32),
        pltpu.VMEM((LT, W), F32),
    ]
    return pl.pallas_call(
        functools.partial(_gdn_kernel, n_lat, n_all, B),
        grid=(B + 1,),
        in_specs=[
            pl.BlockSpec((1, LT, 3 * W), lambda b: (jnp.minimum(b, B - 1), 0, 0)),
            pl.BlockSpec((1, LT, LANE), lambda b: (jnp.minimum(b, B - 1), 0, 0)),
        ],
        out_specs=pl.BlockSpec((1, LT, W), lambda b: (jnp.maximum(b - 1, 0), 0, 0)),
        out_shape=jax.ShapeDtypeStruct((B, LT, W), F32),
        scratch_shapes=one_set + one_set,
        compiler_params=pltpu.CompilerParams(dimension_semantics=("arbitrary",), vmem_limit_bytes=GDN_VMEM_LIMIT),
        name="gdn",
    )(qkv, bg)


def _merge_kernel(final, x_ref, oa_ref, og_ref, sg_ref, gt_ref, ong_ref, gsum_ref, wout_ref, fg_ref, o_ref):
    wa = oa_ref.shape[2]
    subs = [slice(TOK_TILE * s, TOK_TILE * (s + 1)) for s in range(oa_ref.shape[1])]
    ogs = [og_ref[0, sl, :] for sl in subs]
    ssq = [_group_sum(og * og, gsum_ref[...]) for og in ogs]
    ms = []
    for s, sl in enumerate(subs):
        ogn = ogs[s] * lax.rsqrt(ssq[s] * (1.0 / GDN_DV) + EPS) * ong_ref[...]
        ma = (jnp.transpose(oa_ref[0, s].astype(F32)) * sg_ref[0, sl, 0:wa].astype(F32)).astype(BF16)
        mg = (ogn * sg_ref[0, sl, wa:].astype(F32)).astype(BF16)
        ms.append((ma, mg))
    ys = [_dot(ma, wout_ref[0:wa, :]) + _dot(mg, wout_ref[wa:, :]) for ma, mg in ms]
    for s, sl in enumerate(subs):
        xn = x_ref[0, sl, :] + gt_ref[0] * ys[s]
        if final:
            xn = xn * lax.rsqrt(jnp.mean(xn * xn, axis=-1, keepdims=True) + EPS) * fg_ref[...]
        o_ref[0, sl, :] = xn


def _merge(x, o_attn, o_gdn, sg, gt, lw, gsum, final_g, row_start, final):
    B, L, D = x.shape
    sub = min(MERGE_SUB, L // TOK_TILE)
    rows = sub * TOK_TILE
    nt = L // rows
    t0 = row_start // rows
    full = lambda a: pl.BlockSpec(a.shape, lambda b, t: (0,) * a.ndim)
    return pl.pallas_call(
        functools.partial(_merge_kernel, final),
        grid=(B, nt),
        in_specs=[
            pl.BlockSpec((1, rows, D), lambda b, t: (b, t, 0)),
            pl.BlockSpec((1, sub, o_attn.shape[2], TOK_TILE), lambda b, t: (b, t, 0, 0)),
            pl.BlockSpec((1, rows, GDN_WIDTH), lambda b, t: (b, t0 + t, 0)),
            pl.BlockSpec((1, rows, D_MIX), lambda b, t: (b, t0 + t, 0)),
            pl.BlockSpec((1, 1, D), lambda b, t: (b, 0, 0)),
            full(lw["on_g"]), full(gsum), full(lw["w_out"]), full(final_g),
        ],
        out_specs=pl.BlockSpec((1, rows, D), lambda b, t: (b, t, 0)),
        out_shape=jax.ShapeDtypeStruct((B, L, D), F32),
        compiler_params=_params("parallel", "arbitrary"),
        name="merge",
    )(x, o_attn, o_gdn, sg, gt, lw["on_g"], gsum, lw["w_out"], final_g)


def _layer_weights(l, w_in, norm_g, mla_qn_g, mla_w_uq, mla_kvn_g, mla_w_ukv, gqa_qn_g, gqa_kn_g,
                   gdn_conv_w, gdn_a_log, gdn_dt_bias, gdn_on_g, w_out):
    D = D_MODEL
    w = w_in[l]
    splits = np.cumsum([0, MLA_Q_RANK, MLA_KV_RANK, MLA_ROPE, GQA_HEADS * GQA_DIM, GQA_KV_HEADS * GQA_DIM,
                        GQA_KV_HEADS * GQA_DIM, 3 * GDN_HEADS * GDN_DK, 2 * GDN_HEADS, 2 * GDN_HEADS, D_MIX])
    seg = [w[:, int(a):int(b)] for a, b in zip(splits[:-1], splits[1:])]
    cq, ckv, kr, gq, gk, gv, gdn, bb, aa, gate = seg
    z = lambda n: jnp.zeros((D, n), w.dtype)
    kr_p = jnp.concatenate([z(MLA_NOPE), kr, z(LANE - MLA_NOPE - MLA_ROPE)], axis=1)
    k0, k1 = gk[:, :GQA_DIM], gk[:, GQA_DIM:]
    gk2 = jnp.concatenate([k0, k1, k1, k0], axis=1)
    ba = jnp.concatenate([bb, aa, z(LANE - 2 * _N_BA)], axis=1)
    w_p = jnp.concatenate([cq, ckv, kr_p, gq, gk2, gdn, ba, gate], axis=1).astype(BF16)

    dqk = MLA_NOPE + MLA_ROPE
    uq = mla_w_uq[l].reshape(MLA_Q_RANK, MLA_HEADS, dqk)
    uq_p = jnp.concatenate([uq, jnp.zeros((MLA_Q_RANK, MLA_HEADS, LANE - dqk), uq.dtype)], axis=2)
    ukv = mla_w_ukv[l].reshape(MLA_KV_RANK, MLA_HEADS, MLA_NOPE + MLA_V)
    uk_p = jnp.concatenate([ukv[:, :, :MLA_NOPE],
                            jnp.zeros((MLA_KV_RANK, MLA_HEADS, LANE - MLA_NOPE), ukv.dtype)], axis=2)
    uv = ukv[:, :, MLA_NOPE:]
    return {
        "norm_g": norm_g[l].reshape(1, D),
        "w_in": w_p,
        "qn_g": mla_qn_g[l].reshape(1, -1),
        "w_uq": uq_p.reshape(MLA_Q_RANK, MLA_HEADS * LANE).astype(BF16),
        "kvn_g": mla_kvn_g[l].reshape(1, -1),
        "w_uk": uk_p.reshape(MLA_KV_RANK, MLA_HEADS * LANE).astype(BF16),
        "w_uv_t": uv.reshape(MLA_KV_RANK, MLA_HEADS * MLA_V).T.astype(BF16),
        "w_gv_t": gv.T.astype(BF16),
        "gq_g": (jnp.tile(gqa_qn_g[l], GQA_HEADS) * (GQA_SCALE * LOG2E)).reshape(1, -1),
        "gk_g": jnp.tile(gqa_kn_g[l], 2 * 2).reshape(1, -1),
        "conv_w": jnp.concatenate([gdn_conv_w[l], jnp.zeros((SUBLANE - GDN_CONV, _W_GDN), F32)], axis=0),
        "alog_x": jnp.concatenate([jnp.zeros((_N_BA,), F32), gdn_a_log[l].reshape(-1),
                                   jnp.zeros((LANE - 2 * _N_BA,), F32)]).reshape(1, LANE),
        "dtb_x": jnp.concatenate([jnp.zeros((_N_BA,), F32), gdn_dt_bias[l].reshape(-1),
                                  jnp.zeros((LANE - 2 * _N_BA,), F32)]).reshape(1, LANE),
        "on_g": jnp.tile(gdn_on_g[l], GDN_HEADS).reshape(1, -1),
        "w_out": w_out[l].astype(BF16),
    }


def _rope_tables(seq_len):
    LT = seq_len + CTX_LEN
    t = np.arange(seq_len)
    rows, cols = (t // GRID_W).astype(np.float64), (t % GRID_W).astype(np.float64)

    def fill(c, s, lane0, d, pos):
        half = d // 2
        inv = ROPE_BASE ** (-np.arange(0, d, 2, dtype=np.float64) / d)
        ang = pos[:, None] * inv[None, :]
        c[:seq_len, lane0:lane0 + half] = np.cos(ang)
        c[:seq_len, lane0 + half:lane0 + d] = np.cos(ang)
        s[:seq_len, lane0:lane0 + half] = -np.sin(ang)
        s[:seq_len, lane0 + half:lane0 + d] = np.sin(ang)

    ca, sa = np.ones((LT, LANE)), np.zeros((LT, LANE))
    fill(ca, sa, MLA_NOPE, MLA_ROPE // 2, rows)
    fill(ca, sa, MLA_NOPE + MLA_ROPE // 2, MLA_ROPE // 2, cols)
    cb, sb = np.ones((LT, LANE)), np.zeros((LT, LANE))
    for h0 in (0, GQA_DIM):
        fill(cb, sb, h0, GQA_DIM // 2, rows)
        fill(cb, sb, h0 + GQA_DIM // 2, GQA_DIM // 2, cols)
    lane = np.arange(3 * LANE)
    gsum = (lane[:, None] // GQA_DIM == lane[None, :] // GQA_DIM).astype(np.float32)
    return {"ca": jnp.asarray(ca, F32), "sa": jnp.asarray(sa, F32),
            "cb": jnp.asarray(cb, F32), "sb": jnp.asarray(sb, F32),
            "gsum": jnp.asarray(gsum, BF16)}


def kernel(x, c, ctx, c_ctx, w_ada, b_ada, norm_g, w_in, mla_qn_g, mla_w_uq, mla_kvn_g, mla_w_ukv,
           gqa_qn_g, gqa_kn_g, gdn_conv_w, gdn_a_log, gdn_dt_bias, gdn_on_g, w_out, final_g):
    B, L, D = x.shape
    assert D == D_MODEL and L % TOK_TILE == 0 and ctx.shape[1] == CTX_LEN
    tabs = _rope_tables(L)
    rows = -(-(B + 1) // SUBLANE) * SUBLANE
    cvec = jnp.concatenate([c, c_ctx[None, :], jnp.zeros((rows - B - 1, D), F32)], axis=0)
    mod = _adaln(cvec, w_ada, b_ada)
    fg = final_g.reshape(1, D)
    gs256 = tabs["gsum"][:GDN_WIDTH, :GDN_WIDTH]

    for l in range(DEPTH):
        lw = _layer_weights(l, w_in, norm_g, mla_qn_g, mla_w_uq, mla_kvn_g, mla_w_ukv, gqa_qn_g,
                            gqa_kn_g, gdn_conv_w, gdn_a_log, gdn_dt_bias, gdn_on_g, w_out)
        ml = mod[l, :B].reshape(B, 1, 3 * D)
        mc = mod[l, B].reshape(1, 3 * D)
        shl, scl, gtl = ml[..., :D], ml[..., D:2 * D], ml[..., 2 * D:]
        shc, scc, gtc = mc[:, :D], mc[:, D:2 * D], mc[:, 2 * D:]
        q, k, v_t, qkv, bg, sg = _inproj(x, ctx, shl, scl, shc, scc, lw, tabs)
        o_gdn = _gdn(qkv, bg, L)
        o_lat = _attention(q, k, v_t, 0, L, 0, L + CTX_LEN)
        last = l == DEPTH - 1
        x_new = _merge(x, o_lat, o_gdn, sg, gtl, lw, gs256, fg, 0, last)
        if not last:
            o_ctx = _attention(q, k, v_t, L, CTX_LEN, L, CTX_LEN)
            gtc_b = jnp.broadcast_to(gtc.reshape(1, 1, D), (B, 1, D))
            ctx = _merge(ctx, o_ctx, o_gdn, sg, gtc_b, lw, gs256, fg, L, False)
        x = x_new
    return x
```

```python
import functools
import math

import numpy as np
import jax
import jax.numpy as jnp
from jax import lax
from jax.experimental import pallas as pl
from jax.experimental.pallas import tpu as pltpu

F32 = jnp.float32
BF16 = jnp.bfloat16
HIGHEST = lax.Precision.HIGHEST

D_MODEL = 1024
DEPTH = 2
CTX_LEN = 256
GRID_W = 64
D_MIX = 1024
EPS = 1e-6
ROPE_BASE = 10000.0

MLA_HEADS = 6
MLA_NOPE = 64
MLA_ROPE = 32
MLA_V = 64
MLA_Q_RANK = 384
MLA_KV_RANK = 256
LOG2E = math.log2(math.e)
MLA_SCALE = (MLA_NOPE + MLA_ROPE) ** -0.5

GQA_HEADS = 6
GQA_KV_HEADS = 2
GQA_DIM = 64
GQA_SCALE = GQA_DIM ** -0.5

GDN_HEADS = 4
GDN_DK = 64
GDN_DV = 64
GDN_WIDTH = GDN_HEADS * GDN_DV
GDN_CONV = 5
GDN_CHUNK = 64

LANE = 128
SUBLANE = 8
TOK_TILE = 256
ATTN_QK_SPLIT = 2
ATTN_TILES_PER_STEP = 4
INPROJ_SUB = 3
MERGE_SUB = 4
GDN_UNROLL = 6
VMEM_LIMIT = 48 * 1024 * 1024

_W_CQ = MLA_Q_RANK
_W_CKV = MLA_KV_RANK
_W_KR = LANE
_W_GQ = GQA_HEADS * GQA_DIM
_W_GK2 = 2 * LANE
_W_GDN = 3 * GDN_WIDTH
_W_BA = LANE
_N_BA = 2 * GDN_HEADS
_W_GATE = D_MIX
_OFFS = np.cumsum([0, _W_CQ, _W_CKV, _W_KR, _W_GQ, _W_GK2, _W_GDN, _W_BA, _W_GATE])
(_O_CQ, _O_CKV, _O_KR, _O_GQ, _O_GK2, _O_GDN, _O_BA, _O_GATE, D_IN_P) = [int(v) for v in _OFFS]


def _dot(a, b):
    return jnp.dot(a, b, preferred_element_type=F32)


def _dot_nt(a, b):
    return lax.dot_general(a, b, (((1,), (1,)), ((), ())), preferred_element_type=F32)


def _params(*sem):
    return pltpu.CompilerParams(dimension_semantics=sem, vmem_limit_bytes=VMEM_LIMIT)


def _adaln_kernel(c_ref, w_ref, b_ref, o_ref):
    c = c_ref[...]
    s = c * jax.nn.sigmoid(c)
    o_ref[0] = jnp.dot(s, w_ref[0], precision=HIGHEST, preferred_element_type=F32) + b_ref[0]


def _adaln(cvec, w_ada, b_ada):
    rows = cvec.shape[0]
    nblk = w_ada.shape[2] // D_MODEL
    return pl.pallas_call(
        _adaln_kernel,
        grid=(DEPTH, nblk),
        in_specs=[
            pl.BlockSpec((rows, D_MODEL), lambda l, j: (0, 0)),
            pl.BlockSpec((1, D_MODEL, D_MODEL), lambda l, j: (l, 0, j)),
            pl.BlockSpec((1, 1, D_MODEL), lambda l, j: (l, 0, j)),
        ],
        out_specs=pl.BlockSpec((1, rows, D_MODEL), lambda l, j: (l, 0, j)),
        out_shape=jax.ShapeDtypeStruct((DEPTH, rows, 3 * D_MODEL), F32),
        compiler_params=_params("arbitrary", "arbitrary"),
        name="adaln",
    )(cvec, w_ada, b_ada.reshape(DEPTH, 1, 3 * D_MODEL))


def _group_sum(sq, gsum):
    return _dot(sq.astype(BF16), gsum)


def _rope_block(xb, c, s, half):
    lane = lax.broadcasted_iota(jnp.int32, xb.shape, 1)
    up = pltpu.roll(xb, LANE - half, 1)
    dn = pltpu.roll(xb, half, 1)
    partner = jnp.where((lane % (2 * half)) < half, up, dn)
    return xb * c + partner * s


def _inproj_kernel(nt_lat, ns, *refs):
    x_refs, xp_refs, xn_refs = refs[0:ns], refs[ns:2 * ns], refs[2 * ns:3 * ns]
    (ctx_ref, shl_ref, scl_ref, shc_ref, scc_ref, ng_ref, win_ref, qng_ref, wuq_ref, kvng_ref, wuk_ref,
     wuvt_ref, wgvt_ref, gqg_ref, gkg_ref, cw_ref, alog_ref, dtb_ref, ca_ref, sa_ref, cb_ref, sb_ref,
     gsum_ref, q_ref, k_ref, vt_ref, qkv_ref, bg_ref, sg_ref) = refs[3 * ns:]
    t = pl.program_id(1)
    tm = TOK_TILE
    nv = MLA_HEADS * MLA_V
    gsum = gsum_ref[...]
    lane = lax.broadcasted_iota(jnp.int32, (tm, LANE), 1)
    row = lax.broadcasted_iota(jnp.int32, (tm + 2 * SUBLANE, 1), 0)
    tiles = [ns * t + s for s in range(ns)]
    rows = [slice(tm * s, tm * (s + 1)) for s in range(ns)]

    hbe, hb = [], []
    for s in range(ns):
        is_ctx = tiles[s] == nt_lat
        xt = jnp.where(is_ctx, ctx_ref[0], x_refs[s][0])
        xe = jnp.concatenate([xp_refs[s][0], xt, xn_refs[s][0]], axis=0)
        sh = jnp.where(is_ctx, shc_ref[...], shl_ref[0])
        sc = jnp.where(is_ctx, scc_ref[...], scl_ref[0])
        ms = jnp.mean(xe * xe, axis=-1, keepdims=True)
        he = (xe * lax.rsqrt(ms + EPS) * ng_ref[...]) * (1.0 + sc) + sh
        hbe.append(he.astype(BF16))
        hb.append(he[SUBLANE:SUBLANE + tm].astype(BF16))

    z = []
    for s in range(ns):
        zs = dict(
            g=_dot(hbe[s], win_ref[:, _O_GDN:_O_GDN + _W_GDN]),
            cq=_dot(hb[s], win_ref[:, _O_CQ:_O_CQ + _W_CQ]),
            ckv=_dot(hb[s], win_ref[:, _O_CKV:_O_CKV + _W_CKV]),
            q=_dot(hb[s], win_ref[:, _O_GQ:_O_GQ + _W_GQ]),
            k=_dot(hb[s], win_ref[:, _O_GK2:_O_GK2 + _W_GK2]),
            kr=_dot(hb[s], win_ref[:, _O_KR:_O_KR + _W_KR]),
            ba=_dot(hb[s], win_ref[:, _O_BA:_O_BA + _W_BA]))
        vg = _dot_nt(wgvt_ref[...], hb[s]).astype(BF16)
        for j in range(GQA_HEADS):
            kv = j // (GQA_HEADS // GQA_KV_HEADS)
            vt_ref[0, nv + GQA_DIM * j:nv + GQA_DIM * (j + 1), rows[s]] = vg[GQA_DIM * kv:GQA_DIM * (kv + 1)]
        zs["gate"] = _dot(hb[s], win_ref[:, _O_GATE:_O_GATE + _W_GATE])
        z.append(zs)

    for s in range(ns):
        zs, rs = z[s], rows[s]
        is_ctx = tiles[s] == nt_lat
        ca, sa = ca_ref[rs, :], sa_ref[rs, :]
        cb, sb = cb_ref[rs, :], sb_ref[rs, :]

        has_prev = jnp.logical_and(tiles[s] > 0, jnp.logical_not(is_ctx))
        has_next = tiles[s] < nt_lat - 1
        keep = jnp.logical_and(jnp.logical_or(row >= SUBLANE, has_prev),
                               jnp.logical_or(row < SUBLANE + tm, has_next))
        zg = jnp.where(keep, zs["g"], 0.0)
        pad = GDN_CONV // 2
        y = jnp.zeros((tm, _W_GDN), F32)
        for j in range(GDN_CONV):
            o = SUBLANE - pad + j
            y = y + zg[o:o + tm] * cw_ref[j:j + 1, :]
        y = y * jax.nn.sigmoid(y)
        gs = gsum[:GDN_WIDTH, :GDN_WIDTH]
        qg = y[:, :GDN_WIDTH]
        kg = y[:, GDN_WIDTH:2 * GDN_WIDTH]
        qkv_ref[0, rs, 0:GDN_WIDTH] = qg * lax.rsqrt(_group_sum(qg * qg, gs) + EPS) * (GDN_DK ** -0.5)
        qkv_ref[0, rs, GDN_WIDTH:2 * GDN_WIDTH] = kg * lax.rsqrt(_group_sum(kg * kg, gs) + EPS)
        qkv_ref[0, rs, 2 * GDN_WIDTH:] = y[:, 2 * GDN_WIDTH:]
        za = zs["ba"] + dtb_ref[...]
        softplus = jnp.maximum(za, 0.0) + jnp.log(1.0 + jnp.exp(-jnp.abs(za)))
        bg_ref[0, rs, :] = jnp.where(lane < _N_BA, jax.nn.sigmoid(zs["ba"]), -jnp.exp(alog_ref[...]) * softplus)

        zcq = zs["cq"]
        cqn = zcq * lax.rsqrt(jnp.mean(zcq * zcq, axis=-1, keepdims=True) + EPS) * qng_ref[...]
        q = _dot(cqn.astype(BF16), wuq_ref[...]) * (MLA_SCALE * LOG2E)
        for h in range(MLA_HEADS):
            sl = slice(LANE * h, LANE * (h + 1))
            q_ref[0, h, rs, :] = _rope_block(q[:, sl], ca, sa, MLA_ROPE // 4).astype(BF16)

        zckv = zs["ckv"]
        ckvn = (zckv * lax.rsqrt(jnp.mean(zckv * zckv, axis=-1, keepdims=True) + EPS) * kvng_ref[...]).astype(BF16)
        kr = _rope_block(zs["kr"], ca, sa, MLA_ROPE // 4)
        kn = _dot(ckvn, wuk_ref[...])
        for h in range(MLA_HEADS):
            sl = slice(LANE * h, LANE * (h + 1))
            k_ref[0, h, rs, :] = (kn[:, sl] + kr).astype(BF16)
        vt_ref[0, 0:nv, rs] = _dot_nt(wuvt_ref[...], ckvn).astype(BF16)

        zq, zk = zs["q"], zs["k"]
        qn = zq * lax.rsqrt(_group_sum(zq * zq, gsum) * (1.0 / GQA_DIM) + EPS) * gqg_ref[...]
        for p in range(GQA_HEADS // 2):
            blk = _rope_block(qn[:, LANE * p:LANE * (p + 1)], cb, sb, GQA_DIM // 4)
            q_ref[0, MLA_HEADS + 2 * p, rs, :] = jnp.where(lane < GQA_DIM, blk, 0.0).astype(BF16)
            q_ref[0, MLA_HEADS + 2 * p + 1, rs, :] = jnp.where(lane >= GQA_DIM, blk, 0.0).astype(BF16)
        kn2 = zk * lax.rsqrt(_group_sum(zk * zk, gsum[:_W_GK2, :_W_GK2]) * (1.0 / GQA_DIM) + EPS) * gkg_ref[...]
        kblks = [_rope_block(kn2[:, LANE * p:LANE * (p + 1)], cb, sb, GQA_DIM // 4).astype(BF16) for p in range(2)]
        for j in range(GQA_HEADS):
            kv = j // (GQA_HEADS // GQA_KV_HEADS)
            k_ref[0, MLA_HEADS + j, rs, :] = kblks[0 if (j % 2) == kv else 1]

        zgate = zs["gate"]
        sg_ref[0, rs, :] = (zgate * jax.nn.sigmoid(zgate)).astype(BF16)


def _inproj(x, ctx, shl, scl, shc, scc, lw, tabs):
    B, L, D = x.shape
    ns = INPROJ_SUB
    nt_lat = L // TOK_TILE
    nt = nt_lat + CTX_LEN // TOK_TILE
    assert nt % ns == 0
    LT = L + CTX_LEN
    rb = TOK_TILE // SUBLANE
    nrb = L // SUBLANE
    blk = ns * TOK_TILE
    full = lambda a: pl.BlockSpec(a.shape, lambda b, t: (0,) * a.ndim)
    tok = lambda w: pl.BlockSpec((1, blk, w), lambda b, t: (b, t, 0))
    tab = pl.BlockSpec((blk, LANE), lambda b, t: (t, 0))
    mod = pl.BlockSpec((1, 1, D), lambda b, t: (b, 0, 0))
    consts = [lw["norm_g"], lw["w_in"], lw["qn_g"], lw["w_uq"], lw["kvn_g"], lw["w_uk"], lw["w_uv_t"], lw["w_gv_t"],
              lw["gq_g"], lw["gk_g"], lw["conv_w"], lw["alog_x"], lw["dtb_x"]]
    x_specs = [pl.BlockSpec((1, TOK_TILE, D), lambda b, t, s=s: (b, jnp.minimum(ns * t + s, nt_lat - 1), 0))
               for s in range(ns)]
    xp_specs = [pl.BlockSpec((1, SUBLANE, D), lambda b, t, s=s: (b, jnp.clip((ns * t + s) * rb - 1, 0, nrb - 1), 0))
                for s in range(ns)]
    xn_specs = [pl.BlockSpec((1, SUBLANE, D), lambda b, t, s=s: (b, jnp.clip((ns * t + s + 1) * rb, 0, nrb - 1), 0))
                for s in range(ns)]
    in_specs = x_specs + xp_specs + xn_specs + [
        pl.BlockSpec((1, CTX_LEN, D), lambda b, t: (b, 0, 0)),
        mod, mod, full(shc), full(scc),
    ] + [full(a) for a in consts] + [tab, tab, tab, tab, full(tabs["gsum"])]
    nh = MLA_HEADS + GQA_HEADS
    head_major = pl.BlockSpec((1, nh, blk, LANE), lambda b, t: (b, 0, t, 0))
    out_w = [(_W_GDN, F32), (_W_BA, F32), (D_MIX, BF16)]
    out_specs = [head_major, head_major, pl.BlockSpec((1, nh * MLA_V, blk), lambda b, t: (b, 0, t))
                 ] + [tok(w) for w, _ in out_w]
    out_shape = [jax.ShapeDtypeStruct((B, nh, LT, LANE), BF16), jax.ShapeDtypeStruct((B, nh, LT, LANE), BF16),
                 jax.ShapeDtypeStruct((B, nh * MLA_V, LT), BF16)
                 ] + [jax.ShapeDtypeStruct((B, LT, w), dt) for w, dt in out_w]
    return pl.pallas_call(
        functools.partial(_inproj_kernel, nt_lat, ns),
        grid=(B, nt // ns),
        in_specs=in_specs,
        out_specs=out_specs,
        out_shape=out_shape,
        compiler_params=_params("parallel", "arbitrary"),
        name="inproj",
    )(*([x] * (3 * ns)), ctx, shl, scl, shc, scc, *consts,
      tabs["ca"], tabs["sa"], tabs["cb"], tabs["sb"], tabs["gsum"])


def _attn_kernel(q_ref, k_ref, vt_ref, o_ref, s0_scr, s1_scr, p0_scr, p1_scr):
    nh = q_ref.shape[1]
    tq = s0_scr.shape[1]
    n_items = nh * (q_ref.shape[2] // tq)
    assert n_items % 2 == 0
    dv = vt_ref.shape[1] // nh
    ones_rows = jnp.ones((2 * SUBLANE, k_ref.shape[2]), BF16)
    s_scr, p_scr = (s0_scr, s1_scr), (p0_scr, p1_scr)

    def scores(i, par):
        tile, h = i // nh, i % nh
        q = q_ref[0, h, pl.ds(pl.multiple_of(tile * tq, tq), tq), :]
        lk = k_ref.shape[2]
        half = lk // ATTN_QK_SPLIT if lk % (ATTN_QK_SPLIT * 2 * SUBLANE) == 0 else lk
        m = None
        for lo in range(0, lk, half):
            s = _dot_nt(k_ref[0, h, lo:lo + half, :], q)
            s_scr[par][lo:lo + half, :] = s
            mh = jnp.max(s, axis=0, keepdims=True)
            m = mh if m is None else jnp.maximum(m, mh)
        return m

    def probs(par, m):
        p_scr[par][...] = jnp.exp2(s_scr[par][...] - m).astype(BF16)

    def values(i, par):
        tile, h = i // nh, i % nh
        r0 = pl.multiple_of(h * dv, dv)
        lhs = jnp.concatenate([vt_ref[0, pl.ds(r0, dv), :], ones_rows], axis=0)
        r = _dot(lhs, p_scr[par][...])
        o_ref[0, tile, pl.ds(r0, dv), :] = (r[0:dv] / r[dv:dv + 1]).astype(o_ref.dtype)

    def stage(i, par, m_i):
        values(i - 1, 1 - par)
        m_next = scores(i + 1, 1 - par)
        probs(par, m_i)
        return m_next

    m0 = scores(0, 0)
    m1 = scores(1, 1)
    probs(0, m0)

    def body(j, m_odd):
        i = 2 * j + 1
        m_even = stage(i, 1, m_odd)
        return stage(i + 1, 0, m_even)

    m_last = lax.fori_loop(0, (n_items - 2) // 2, body, m1)
    values(n_items - 2, 0)
    probs(1, m_last)
    values(n_items - 1, 1)


def _attention(q, k, v_t, q_start, q_len, k_start, k_len):
    B, nh = q.shape[0], q.shape[1]
    tq = TOK_TILE
    tiles = min(ATTN_TILES_PER_STEP, q_len // tq)
    qb = tiles * tq
    nq = q_len // qb
    q0 = q_start // qb
    kblk = k_start // k_len
    wo = v_t.shape[1]
    return pl.pallas_call(
        _attn_kernel,
        grid=(B, nq),
        in_specs=[pl.BlockSpec((1, nh, qb, LANE), lambda b, i: (b, 0, q0 + i, 0)),
                  pl.BlockSpec((1, nh, k_len, LANE), lambda b, i: (b, 0, kblk, 0)),
                  pl.BlockSpec((1, wo, k_len), lambda b, i: (b, 0, kblk))],
        out_specs=pl.BlockSpec((1, tiles, wo, tq), lambda b, i: (b, i, 0, 0)),
        out_shape=jax.ShapeDtypeStruct((B, q_len // tq, wo, tq), BF16),
        scratch_shapes=[pltpu.VMEM((k_len, tq), F32), pltpu.VMEM((k_len, tq), F32),
                        pltpu.VMEM((k_len, tq), BF16), pltpu.VMEM((k_len, tq), BF16)],
        compiler_params=_params("parallel", "arbitrary"),
        name="attention",
    )(q, k, v_t)


def _gdn_kernel(n_lat, n_all, qkv_ref, bg_ref, o_ref, mp_ref, qp_ref, cc_ref, gl_ref):
    C, W, U = GDN_CHUNK, GDN_WIDTH, GDN_UNROLL
    n_ctx = n_all - n_lat

    ri = lax.broadcasted_iota(jnp.int32, (C, W), 0)
    cj = lax.broadcasted_iota(jnp.int32, (C, W), 1) % C
    lane_head = lax.broadcasted_iota(jnp.int32, (C, W), 1) // GDN_DV
    diag = ri == cj
    eye_lb = jnp.where(diag, 1.0, 0.0).astype(F32)
    blk4 = jnp.where(ri // 4 == cj // 4, 1.0, 0.0).astype(F32)
    lvl_masks = [jnp.where(jnp.logical_and(ri // (2 * bs) == cj // (2 * bs), ri // bs != cj // bs), 1.0, 0.0
                           ).astype(F32) for bs in (4, 8, 16, 32)]
    r2 = lax.broadcasted_iota(jnp.int32, (C, C), 0)
    c2 = lax.broadcasted_iota(jnp.int32, (C, C), 1)
    ones8 = jnp.ones((SUBLANE, C), F32)
    rb = lax.broadcasted_iota(jnp.int32, (GDN_HEADS * C, W), 0) // C
    cbk = lax.broadcasted_iota(jnp.int32, (GDN_HEADS * C, W), 1) // C
    bmask = jnp.where(rb == cbk, 1.0, 0.0).astype(BF16)
    bmask_f = bmask.astype(F32)
    incl = [ri >= cj, ri <= cj]
    incl_t = [jnp.where(cj >= ri, 1.0, 0.0).astype(F32), jnp.where(cj <= ri, 1.0, 0.0).astype(F32)]
    tri = [jnp.where(r2 >= c2, 1.0, 0.0).astype(F32), jnp.where(r2 <= c2, 1.0, 0.0).astype(F32)]
    last_row = [C - 1, 0]

    def bd(y):
        yb = y.astype(BF16)
        return jnp.concatenate([yb] * GDN_HEADS, axis=0) * bmask

    def mm(x, ybd):
        return _dot(x.astype(BF16), ybd)

    def diag_blocks(full):
        acc = full[0:C]
        for h in range(1, GDN_HEADS):
            acc = acc + full[C * h:C * (h + 1)]
        return acc

    def expand(bg, col0):
        out = jnp.broadcast_to(bg[:, col0:col0 + 1], (C, W))
        for h in range(1, GDN_HEADS):
            out = jnp.where(lane_head == h, jnp.broadcast_to(bg[:, col0 + h:col0 + h + 1], (C, W)), out)
        return out

    def phase1(it, carry):
        probs = [(c, d) for c in range(U) for d in range(2)]
        each = lambda f, *ls: [f(*a) for a in zip(*ls)]
        per = lambda f: [f(c, d) for c, d in probs]
        rows = [pl.multiple_of((it * U + c) * C, C) for c in range(U)]
        q = [qkv_ref[0, pl.ds(r, C), 0:W] for r in rows]
        k = [qkv_ref[0, pl.ds(r, C), W:2 * W] for r in rows]
        v = [qkv_ref[0, pl.ds(r, C), 2 * W:3 * W] for r in rows]
        bg = [bg_ref[0, pl.ds(r, C), :] for r in rows]
        kbf = each(lambda x: x.astype(BF16), k)
        gram = each(lambda kb_, q_: _dot_nt(jnp.concatenate([kb_, q_.astype(BF16)], axis=0),
                                            jnp.concatenate([kb_] * GDN_HEADS, axis=0) * bmask), kbf, q)
        beta = per(lambda c, d: expand(bg[c], GDN_HEADS * d))
        g = per(lambda c, d: expand(bg[c], _N_BA + GDN_HEADS * d))
        dirs = [d for _, d in probs]
        chunk = [c for c, _ in probs]
        gc = each(lambda x, d: jnp.dot(tri[d], x, precision=HIGHEST, preferred_element_type=F32), g, dirs)
        g2 = each(lambda x, d: jnp.dot(ones8, x * incl_t[d], precision=HIGHEST, preferred_element_type=F32), g, dirs)
        dlt = each(lambda a, b: a - jnp.broadcast_to(b[0:1], (C, W)), gc, g2)
        dec = each(lambda x, d: jnp.where(incl[d], jnp.exp(jnp.where(incl[d], x, 0.0)), 0.0), dlt, dirs)
        gcl = each(lambda x, d: x[last_row[d]:last_row[d] + 1], gc, dirs)
        egc = each(jnp.exp, gc)
        a_mat = each(lambda c, b, de: gram[c][0:C] * b * jnp.where(diag, 0.0, de), chunk, beta, dec)
        a_intra = each(lambda c, de: (gram[c][C:2 * C] * de).astype(BF16), chunk, dec)

        b_0 = each(lambda a: -(a * blk4), a_mat)
        p_0 = each(lambda b: b + mm(b, bd(b)), b_0)
        t_mat = each(lambda b, p: eye_lb + b + mm(p, bd(b)), b_0, p_0)
        for msk in lvl_masks:
            y = each(lambda a, x: mm(a * msk, bd(x)), a_mat, t_mat)
            t_mat = each(lambda x, y_: x - mm(x, bd(y_)), t_mat, y)

        u = each(lambda t, c, b: mm(t, bd(v[c] * b)), t_mat, chunk, beta)
        w = each(lambda t, c, b, e: mm(t, bd(k[c] * b * e)), t_mat, chunk, beta, egc)
        kd_t = each(lambda c, gl_, gc_: jnp.transpose(k[c] * jnp.exp(gl_ - gc_)).astype(BF16), chunk, gcl, gc)
        kwu = each(lambda kt, w_, u_: _dot(kt, jnp.concatenate([w_, u_], axis=1).astype(BF16)), kd_t, w, u)
        aw = each(lambda a, w_: _dot(a, bd(w_)), a_intra, w)
        au = each(lambda a, u_: _dot(a, bd(u_)), a_intra, u)
        for p, (c, d) in enumerate(probs):
            n = it * U + c
            mp_ref[d, n] = (-diag_blocks(kwu[p][:, 0:W] * bmask_f)).astype(BF16)
            cc_ref[d, n] = diag_blocks(kwu[p][:, W:2 * W] * bmask_f)
            qp_ref[d, n] = (q[c] * egc[p] - aw[p]).astype(BF16)
            gl_ref[d, n] = jnp.broadcast_to(jnp.exp(gcl[p]), (SUBLANE, W))
        for c in range(U):
            o_ref[0, pl.ds(rows[c], C), :] = au[2 * c] + au[2 * c + 1]
        return carry

    assert n_all % U == 0
    lax.fori_loop(0, n_all // U, phase1, 0)

    def phase2(i, carry):
        n = (jnp.where(i < n_ctx, n_lat + i, i - n_ctx), n_all - 1 - i)
        r = [_dot(jnp.concatenate([mp_ref[d, n[d]], qp_ref[d, n[d]]], axis=0), bd(carry[d])) for d in range(2)]
        new = []
        for d in range(2):
            r0 = pl.multiple_of(n[d] * C, C)
            o_ref[0, pl.ds(r0, C), :] += r[d][C:2 * C]
            new.append(gl_ref[d, n[d]][0:1] * carry[d] + r[d][0:C] + cc_ref[d, n[d]])
        return tuple(new)

    zero = jnp.zeros((C, W), F32)
    lax.fori_loop(0, n_all, phase2, (zero, zero))


def _gdn(qkv, bg, n_lat_tokens):
    B, LT, _ = qkv.shape
    n_all = LT // GDN_CHUNK
    n_lat = n_lat_tokens // GDN_CHUNK
    W = GDN_WIDTH
    return pl.pallas_call(
        functools.partial(_gdn_kernel, n_lat, n_all),
        grid=(B,),
        in_specs=[
            pl.BlockSpec((1, LT, 3 * W), lambda b: (b, 0, 0)),
            pl.BlockSpec((1, LT, LANE), lambda b: (b, 0, 0)),
        ],
        out_specs=pl.BlockSpec((1, LT, W), lambda b: (b, 0, 0)),
        out_shape=jax.ShapeDtypeStruct((B, LT, W), F32),
        scratch_shapes=[
            pltpu.VMEM((2, n_all, GDN_CHUNK, W), BF16),
            pltpu.VMEM((2, n_all, GDN_CHUNK, W), BF16),
            pltpu.VMEM((2, n_all, GDN_CHUNK, W), F32),
            pltpu.VMEM((2, n_all, SUBLANE, W), F32),
        ],
        compiler_params=_params("parallel"),
        name="gdn",
    )(qkv, bg)


def _merge_kernel(final, x_ref, oa_ref, og_ref, sg_ref, gt_ref, ong_ref, gsum_ref, wout_ref, fg_ref, o_ref):
    wa = oa_ref.shape[2]
    subs = [slice(TOK_TILE * s, TOK_TILE * (s + 1)) for s in range(oa_ref.shape[1])]
    ogs = [og_ref[0, sl, :] for sl in subs]
    ssq = [_group_sum(og * og, gsum_ref[...]) for og in ogs]
    ms = []
    for s, sl in enumerate(subs):
        ogn = ogs[s] * lax.rsqrt(ssq[s] * (1.0 / GDN_DV) + EPS) * ong_ref[...]
        ma = (jnp.transpose(oa_ref[0, s].astype(F32)) * sg_ref[0, sl, 0:wa].astype(F32)).astype(BF16)
        mg = (ogn * sg_ref[0, sl, wa:].astype(F32)).astype(BF16)
        ms.append((ma, mg))
    ys = [_dot(ma, wout_ref[0:wa, :]) + _dot(mg, wout_ref[wa:, :]) for ma, mg in ms]
    for s, sl in enumerate(subs):
        xn = x_ref[0, sl, :] + gt_ref[0] * ys[s]
        if final:
            xn = xn * lax.rsqrt(jnp.mean(xn * xn, axis=-1, keepdims=True) + EPS) * fg_ref[...]
        o_ref[0, sl, :] = xn


def _merge(x, o_attn, o_gdn, sg, gt, lw, gsum, final_g, row_start, final):
    B, L, D = x.shape
    sub = min(MERGE_SUB, L // TOK_TILE)
    rows = sub * TOK_TILE
    nt = L // rows
    t0 = row_start // rows
    full = lambda a: pl.BlockSpec(a.shape, lambda b, t: (0,) * a.ndim)
    return pl.pallas_call(
        functools.partial(_merge_kernel, final),
        grid=(B, nt),
        in_specs=[
            pl.BlockSpec((1, rows, D), lambda b, t: (b, t, 0)),
            pl.BlockSpec((1, sub, o_attn.shape[2], TOK_TILE), lambda b, t: (b, t, 0, 0)),
            pl.BlockSpec((1, rows, GDN_WIDTH), lambda b, t: (b, t0 + t, 0)),
            pl.BlockSpec((1, rows, D_MIX), lambda b, t: (b, t0 + t, 0)),
            pl.BlockSpec((1, 1, D), lambda b, t: (b, 0, 0)),
            full(lw["on_g"]), full(gsum), full(lw["w_out"]), full(final_g),
        ],
        out_specs=pl.BlockSpec((1, rows, D), lambda b, t: (b, t, 0)),
        out_shape=jax.ShapeDtypeStruct((B, L, D), F32),
        compiler_params=_params("parallel", "arbitrary"),
        name="merge",
    )(x, o_attn, o_gdn, sg, gt, lw["on_g"], gsum, lw["w_out"], final_g)


def _layer_weights(l, w_in, norm_g, mla_qn_g, mla_w_uq, mla_kvn_g, mla_w_ukv, gqa_qn_g, gqa_kn_g,
                   gdn_conv_w, gdn_a_log, gdn_dt_bias, gdn_on_g, w_out):
    D = D_MODEL
    w = w_in[l]
    splits = np.cumsum([0, MLA_Q_RANK, MLA_KV_RANK, MLA_ROPE, GQA_HEADS * GQA_DIM, GQA_KV_HEADS * GQA_DIM,
                        GQA_KV_HEADS * GQA_DIM, 3 * GDN_HEADS * GDN_DK, 2 * GDN_HEADS, 2 * GDN_HEADS, D_MIX])
    seg = [w[:, int(a):int(b)] for a, b in zip(splits[:-1], splits[1:])]
    cq, ckv, kr, gq, gk, gv, gdn, bb, aa, gate = seg
    z = lambda n: jnp.zeros((D, n), w.dtype)
    kr_p = jnp.concatenate([z(MLA_NOPE), kr, z(LANE - MLA_NOPE - MLA_ROPE)], axis=1)
    k0, k1 = gk[:, :GQA_DIM], gk[:, GQA_DIM:]
    gk2 = jnp.concatenate([k0, k1, k1, k0], axis=1)
    ba = jnp.concatenate([bb, aa, z(LANE - 2 * _N_BA)], axis=1)
    w_p = jnp.concatenate([cq, ckv, kr_p, gq, gk2, gdn, ba, gate], axis=1).astype(BF16)

    dqk = MLA_NOPE + MLA_ROPE
    uq = mla_w_uq[l].reshape(MLA_Q_RANK, MLA_HEADS, dqk)
    uq_p = jnp.concatenate([uq, jnp.zeros((MLA_Q_RANK, MLA_HEADS, LANE - dqk), uq.dtype)], axis=2)
    ukv = mla_w_ukv[l].reshape(MLA_KV_RANK, MLA_HEADS, MLA_NOPE + MLA_V)
    uk_p = jnp.concatenate([ukv[:, :, :MLA_NOPE],
                            jnp.zeros((MLA_KV_RANK, MLA_HEADS, LANE - MLA_NOPE), ukv.dtype)], axis=2)
    uv = ukv[:, :, MLA_NOPE:]
    return {
        "norm_g": norm_g[l].reshape(1, D),
        "w_in": w_p,
        "qn_g": mla_qn_g[l].reshape(1, -1),
        "w_uq": uq_p.reshape(MLA_Q_RANK, MLA_HEADS * LANE).astype(BF16),
        "kvn_g": mla_kvn_g[l].reshape(1, -1),
        "w_uk": uk_p.reshape(MLA_KV_RANK, MLA_HEADS * LANE).astype(BF16),
        "w_uv_t": uv.reshape(MLA_KV_RANK, MLA_HEADS * MLA_V).T.astype(BF16),
        "w_gv_t": gv.T.astype(BF16),
        "gq_g": (jnp.tile(gqa_qn_g[l], GQA_HEADS) * (GQA_SCALE * LOG2E)).reshape(1, -1),
        "gk_g": jnp.tile(gqa_kn_g[l], 2 * 2).reshape(1, -1),
        "conv_w": jnp.concatenate([gdn_conv_w[l], jnp.zeros((SUBLANE - GDN_CONV, _W_GDN), F32)], axis=0),
        "alog_x": jnp.concatenate([jnp.zeros((_N_BA,), F32), gdn_a_log[l].reshape(-1),
                                   jnp.zeros((LANE - 2 * _N_BA,), F32)]).reshape(1, LANE),
        "dtb_x": jnp.concatenate([jnp.zeros((_N_BA,), F32), gdn_dt_bias[l].reshape(-1),
                                  jnp.zeros((LANE - 2 * _N_BA,), F32)]).reshape(1, LANE),
        "on_g": jnp.tile(gdn_on_g[l], GDN_HEADS).reshape(1, -1),
        "w_out": w_out[l].astype(BF16),
    }


def _rope_tables(seq_len):
    LT = seq_len + CTX_LEN
    t = np.arange(seq_len)
    rows, cols = (t // GRID_W).astype(np.float64), (t % GRID_W).astype(np.float64)

    def fill(c, s, lane0, d, pos):
        half = d // 2
        inv = ROPE_BASE ** (-np.arange(0, d, 2, dtype=np.float64) / d)
        ang = pos[:, None] * inv[None, :]
        c[:seq_len, lane0:lane0 + half] = np.cos(ang)
        c[:seq_len, lane0 + half:lane0 + d] = np.cos(ang)
        s[:seq_len, lane0:lane0 + half] = -np.sin(ang)
        s[:seq_len, lane0 + half:lane0 + d] = np.sin(ang)

    ca, sa = np.ones((LT, LANE)), np.zeros((LT, LANE))
    fill(ca, sa, MLA_NOPE, MLA_ROPE // 2, rows)
    fill(ca, sa, MLA_NOPE + MLA_ROPE // 2, MLA_ROPE // 2, cols)
    cb, sb = np.ones((LT, LANE)), np.zeros((LT, LANE))
    for h0 in (0, GQA_DIM):
        fill(cb, sb, h0, GQA_DIM // 2, rows)
        fill(cb, sb, h0 + GQA_DIM // 2, GQA_DIM // 2, cols)
    lane = np.arange(3 * LANE)
    gsum = (lane[:, None] // GQA_DIM == lane[None, :] // GQA_DIM).astype(np.float32)
    return {"ca": jnp.asarray(ca, F32), "sa": jnp.asarray(sa, F32),
            "cb": jnp.asarray(cb, F32), "sb": jnp.asarray(sb, F32),
            "gsum": jnp.asarray(gsum, BF16)}


def kernel(x, c, ctx, c_ctx, w_ada, b_ada, norm_g, w_in, mla_qn_g, mla_w_uq, mla_kvn_g, mla_w_ukv,
           gqa_qn_g, gqa_kn_g, gdn_conv_w, gdn_a_log, gdn_dt_bias, gdn_on_g, w_out, final_g):
    B, L, D = x.shape
    assert D == D_MODEL and L % TOK_TILE == 0 and ctx.shape[1] == CTX_LEN
    tabs = _rope_tables(L)
    rows = -(-(B + 1) // SUBLANE) * SUBLANE
    cvec = jnp.concatenate([c, c_ctx[None, :], jnp.zeros((rows - B - 1, D), F32)], axis=0)
    mod = _adaln(cvec, w_ada, b_ada)
    fg = final_g.reshape(1, D)
    gs256 = tabs["gsum"][:GDN_WIDTH, :GDN_WIDTH]

    for l in range(DEPTH):
        lw = _layer_weights(l, w_in, norm_g, mla_qn_g, mla_w_uq, mla_kvn_g, mla_w_ukv, gqa_qn_g,
                            gqa_kn_g, gdn_conv_w, gdn_a_log, gdn_dt_bias, gdn_on_g, w_out)
        ml = mod[l, :B].reshape(B, 1, 3 * D)
        mc = mod[l, B].reshape(1, 3 * D)
        shl, scl, gtl = ml[..., :D], ml[..., D:2 * D], ml[..., 2 * D:]
        shc, scc, gtc = mc[:, :D], mc[:, D:2 * D], mc[:, 2 * D:]
        q, k, v_t, qkv, bg, sg = _inproj(x, ctx, shl, scl, shc, scc, lw, tabs)
        o_gdn = _gdn(qkv, bg, L)
        o_lat = _attention(q, k, v_t, 0, L, 0, L + CTX_LEN)
        last = l == DEPTH - 1
        x_new = _merge(x, o_lat, o_gdn, sg, gtl, lw, gs256, fg, 0, last)
        if not last:
            o_ctx = _attention(q, k, v_t, L, CTX_LEN, L, CTX_LEN)
            gtc_b = jnp.broadcast_to(gtc.reshape(1, 1, D), (B, 1, D))
            ctx = _merge(ctx, o_ctx, o_gdn, sg, gtc_b, lw, gs256, fg, L, False)
        x = x_new
    return x
```

```python
import functools
import math

import numpy as np
import jax
import jax.numpy as jnp
from jax import lax
from jax.experimental import pallas as pl
from jax.experimental.pallas import tpu as pltpu

F32 = jnp.float32
BF16 = jnp.bfloat16
HIGHEST = lax.Precision.HIGHEST

D_MODEL = 1024
DEPTH = 2
CTX_LEN = 256
GRID_W = 64
D_MIX = 1024
EPS = 1e-6
ROPE_BASE = 10000.0

MLA_HEADS = 6
MLA_NOPE = 64
MLA_ROPE = 32
MLA_V = 64
MLA_Q_RANK = 384
MLA_KV_RANK = 256
LOG2E = math.log2(math.e)
MLA_SCALE = (MLA_NOPE + MLA_ROPE) ** -0.5

GQA_HEADS = 6
GQA_KV_HEADS = 2
GQA_DIM = 64
GQA_SCALE = GQA_DIM ** -0.5

GDN_HEADS = 4
GDN_DK = 64
GDN_DV = 64
GDN_WIDTH = GDN_HEADS * GDN_DV
GDN_CONV = 5
GDN_CHUNK = 64

LANE = 128
SUBLANE = 8
TOK_TILE = 256
ATTN_EXP_SPLIT = 2
ATTN_QK_SPLIT = 2
ATTN_TILES_PER_STEP = 4
INPROJ_SUB = 3
MERGE_SUB = 4
GDN_UNROLL = 6
VMEM_LIMIT = 48 * 1024 * 1024

_W_CQ = MLA_Q_RANK
_W_CKV = MLA_KV_RANK
_W_KR = LANE
_W_GQ = GQA_HEADS * GQA_DIM
_W_GK2 = 2 * LANE
_W_GDN = 3 * GDN_WIDTH
_W_BA = LANE
_N_BA = 2 * GDN_HEADS
_W_GATE = D_MIX
_OFFS = np.cumsum([0, _W_CQ, _W_CKV, _W_KR, _W_GQ, _W_GK2, _W_GDN, _W_BA, _W_GATE])
(_O_CQ, _O_CKV, _O_KR, _O_GQ, _O_GK2, _O_GDN, _O_BA, _O_GATE, D_IN_P) = [int(v) for v in _OFFS]


def _dot(a, b):
    return jnp.dot(a, b, preferred_element_type=F32)


def _dot_nt(a, b):
    return lax.dot_general(a, b, (((1,), (1,)), ((), ())), preferred_element_type=F32)


def _params(*sem):
    return pltpu.CompilerParams(dimension_semantics=sem, vmem_limit_bytes=VMEM_LIMIT)


def _adaln_kernel(c_ref, w_ref, b_ref, o_ref):
    c = c_ref[...]
    s = c * jax.nn.sigmoid(c)
    o_ref[0] = jnp.dot(s, w_ref[0], precision=HIGHEST, preferred_element_type=F32) + b_ref[0]


def _adaln(cvec, w_ada, b_ada):
    rows = cvec.shape[0]
    nblk = w_ada.shape[2] // D_MODEL
    return pl.pallas_call(
        _adaln_kernel,
        grid=(DEPTH, nblk),
        in_specs=[
            pl.BlockSpec((rows, D_MODEL), lambda l, j: (0, 0)),
            pl.BlockSpec((1, D_MODEL, D_MODEL), lambda l, j: (l, 0, j)),
            pl.BlockSpec((1, 1, D_MODEL), lambda l, j: (l, 0, j)),
        ],
        out_specs=pl.BlockSpec((1, rows, D_MODEL), lambda l, j: (l, 0, j)),
        out_shape=jax.ShapeDtypeStruct((DEPTH, rows, 3 * D_MODEL), F32),
        compiler_params=_params("arbitrary", "arbitrary"),
        name="adaln",
    )(cvec, w_ada, b_ada.reshape(DEPTH, 1, 3 * D_MODEL))


def _group_sum(sq, gsum):
    return _dot(sq.astype(BF16), gsum)


def _rope_block(xb, c, s, half):
    lane = lax.broadcasted_iota(jnp.int32, xb.shape, 1)
    up = pltpu.roll(xb, LANE - half, 1)
    dn = pltpu.roll(xb, half, 1)
    partner = jnp.where((lane % (2 * half)) < half, up, dn)
    return xb * c + partner * s


def _inproj_kernel(nt_lat, ns, *refs):
    x_refs, xp_refs, xn_refs = refs[0:ns], refs[ns:2 * ns], refs[2 * ns:3 * ns]
    (ctx_ref, shl_ref, scl_ref, shc_ref, scc_ref, ng_ref, win_ref, qng_ref, wuq_ref, kvng_ref, wuk_ref,
     wuvt_ref, wgvt_ref, gqg_ref, gkg_ref, cw_ref, alog_ref, dtb_ref, ca_ref, sa_ref, cb_ref, sb_ref,
     gsum_ref, q_ref, k_ref, vt_ref, qkv_ref, bg_ref, sg_ref) = refs[3 * ns:]
    t = pl.program_id(1)
    tm = TOK_TILE
    nv = MLA_HEADS * MLA_V
    gsum = gsum_ref[...]
    lane = lax.broadcasted_iota(jnp.int32, (tm, LANE), 1)
    row = lax.broadcasted_iota(jnp.int32, (tm + 2 * SUBLANE, 1), 0)
    tiles = [ns * t + s for s in range(ns)]
    rows = [slice(tm * s, tm * (s + 1)) for s in range(ns)]

    hbe, hb = [], []
    for s in range(ns):
        is_ctx = tiles[s] == nt_lat
        xt = jnp.where(is_ctx, ctx_ref[0], x_refs[s][0])
        xe = jnp.concatenate([xp_refs[s][0], xt, xn_refs[s][0]], axis=0)
        sh = jnp.where(is_ctx, shc_ref[...], shl_ref[0])
        sc = jnp.where(is_ctx, scc_ref[...], scl_ref[0])
        ms = jnp.mean(xe * xe, axis=-1, keepdims=True)
        he = (xe * lax.rsqrt(ms + EPS) * ng_ref[...]) * (1.0 + sc) + sh
        hbe.append(he.astype(BF16))
        hb.append(he[SUBLANE:SUBLANE + tm].astype(BF16))

    z = []
    for s in range(ns):
        zs = dict(
            g=_dot(hbe[s], win_ref[:, _O_GDN:_O_GDN + _W_GDN]),
            cq=_dot(hb[s], win_ref[:, _O_CQ:_O_CQ + _W_CQ]),
            ckv=_dot(hb[s], win_ref[:, _O_CKV:_O_CKV + _W_CKV]),
            q=_dot(hb[s], win_ref[:, _O_GQ:_O_GQ + _W_GQ]),
            k=_dot(hb[s], win_ref[:, _O_GK2:_O_GK2 + _W_GK2]),
            kr=_dot(hb[s], win_ref[:, _O_KR:_O_KR + _W_KR]),
            ba=_dot(hb[s], win_ref[:, _O_BA:_O_BA + _W_BA]))
        vg = _dot_nt(wgvt_ref[...], hb[s]).astype(BF16)
        for j in range(GQA_HEADS):
            kv = j // (GQA_HEADS // GQA_KV_HEADS)
            vt_ref[0, nv + GQA_DIM * j:nv + GQA_DIM * (j + 1), rows[s]] = vg[GQA_DIM * kv:GQA_DIM * (kv + 1)]
        zs["gate"] = _dot(hb[s], win_ref[:, _O_GATE:_O_GATE + _W_GATE])
        z.append(zs)

    for s in range(ns):
        zs, rs = z[s], rows[s]
        is_ctx = tiles[s] == nt_lat
        ca, sa = ca_ref[rs, :], sa_ref[rs, :]
        cb, sb = cb_ref[rs, :], sb_ref[rs, :]

        has_prev = jnp.logical_and(tiles[s] > 0, jnp.logical_not(is_ctx))
        has_next = tiles[s] < nt_lat - 1
        keep = jnp.logical_and(jnp.logical_or(row >= SUBLANE, has_prev),
                               jnp.logical_or(row < SUBLANE + tm, has_next))
        zg = jnp.where(keep, zs["g"], 0.0)
        pad = GDN_CONV // 2
        y = jnp.zeros((tm, _W_GDN), F32)
        for j in range(GDN_CONV):
            o = SUBLANE - pad + j
            y = y + zg[o:o + tm] * cw_ref[j:j + 1, :]
        y = y * jax.nn.sigmoid(y)
        gs = gsum[:GDN_WIDTH, :GDN_WIDTH]
        qg = y[:, :GDN_WIDTH]
        kg = y[:, GDN_WIDTH:2 * GDN_WIDTH]
        qkv_ref[0, rs, 0:GDN_WIDTH] = qg * lax.rsqrt(_group_sum(qg * qg, gs) + EPS) * (GDN_DK ** -0.5)
        qkv_ref[0, rs, GDN_WIDTH:2 * GDN_WIDTH] = kg * lax.rsqrt(_group_sum(kg * kg, gs) + EPS)
        qkv_ref[0, rs, 2 * GDN_WIDTH:] = y[:, 2 * GDN_WIDTH:]
        za = zs["ba"] + dtb_ref[...]
        softplus = jnp.maximum(za, 0.0) + jnp.log(1.0 + jnp.exp(-jnp.abs(za)))
        bg_ref[0, rs, :] = jnp.where(lane < _N_BA, jax.nn.sigmoid(zs["ba"]), -jnp.exp(alog_ref[...]) * softplus)

        zcq = zs["cq"]
        cqn = zcq * lax.rsqrt(jnp.mean(zcq * zcq, axis=-1, keepdims=True) + EPS) * qng_ref[...]
        q = _dot(cqn.astype(BF16), wuq_ref[...]) * (MLA_SCALE * LOG2E)
        for h in range(MLA_HEADS):
            sl = slice(LANE * h, LANE * (h + 1))
            q_ref[0, h, rs, :] = _rope_block(q[:, sl], ca, sa, MLA_ROPE // 4).astype(BF16)

        zckv = zs["ckv"]
        ckvn = (zckv * lax.rsqrt(jnp.mean(zckv * zckv, axis=-1, keepdims=True) + EPS) * kvng_ref[...]).astype(BF16)
        kr = _rope_block(zs["kr"], ca, sa, MLA_ROPE // 4)
        kn = _dot(ckvn, wuk_ref[...])
        for h in range(MLA_HEADS):
            sl = slice(LANE * h, LANE * (h + 1))
            k_ref[0, h, rs, :] = (kn[:, sl] + kr).astype(BF16)
        vt_ref[0, 0:nv, rs] = _dot_nt(wuvt_ref[...], ckvn).astype(BF16)

        zq, zk = zs["q"], zs["k"]
        qn = zq * lax.rsqrt(_group_sum(zq * zq, gsum) * (1.0 / GQA_DIM) + EPS) * gqg_ref[...]
        for p in range(GQA_HEADS // 2):
            blk = _rope_block(qn[:, LANE * p:LANE * (p + 1)], cb, sb, GQA_DIM // 4)
            q_ref[0, MLA_HEADS + 2 * p, rs, :] = jnp.where(lane < GQA_DIM, blk, 0.0).astype(BF16)
            q_ref[0, MLA_HEADS + 2 * p + 1, rs, :] = jnp.where(lane >= GQA_DIM, blk, 0.0).astype(BF16)
        kn2 = zk * lax.rsqrt(_group_sum(zk * zk, gsum[:_W_GK2, :_W_GK2]) * (1.0 / GQA_DIM) + EPS) * gkg_ref[...]
        kblks = [_rope_block(kn2[:, LANE * p:LANE * (p + 1)], cb, sb, GQA_DIM // 4).astype(BF16) for p in range(2)]
        for j in range(GQA_HEADS):
            kv = j // (GQA_HEADS // GQA_KV_HEADS)
            k_ref[0, MLA_HEADS + j, rs, :] = kblks[0 if (j % 2) == kv else 1]

        zgate = zs["gate"]
        sg_ref[0, rs, :] = (zgate * jax.nn.sigmoid(zgate)).astype(BF16)


def _inproj(x, ctx, shl, scl, shc, scc, lw, tabs):
    B, L, D = x.shape
    ns = INPROJ_SUB
    nt_lat = L // TOK_TILE
    nt = nt_lat + CTX_LEN // TOK_TILE
    assert nt % ns == 0
    LT = L + CTX_LEN
    rb = TOK_TILE // SUBLANE
    nrb = L // SUBLANE
    blk = ns * TOK_TILE
    full = lambda a: pl.BlockSpec(a.shape, lambda b, t: (0,) * a.ndim)
    tok = lambda w: pl.BlockSpec((1, blk, w), lambda b, t: (b, t, 0))
    tab = pl.BlockSpec((blk, LANE), lambda b, t: (t, 0))
    mod = pl.BlockSpec((1, 1, D), lambda b, t: (b, 0, 0))
    consts = [lw["norm_g"], lw["w_in"], lw["qn_g"], lw["w_uq"], lw["kvn_g"], lw["w_uk"], lw["w_uv_t"], lw["w_gv_t"],
              lw["gq_g"], lw["gk_g"], lw["conv_w"], lw["alog_x"], lw["dtb_x"]]
    x_specs = [pl.BlockSpec((1, TOK_TILE, D), lambda b, t, s=s: (b, jnp.minimum(ns * t + s, nt_lat - 1), 0))
               for s in range(ns)]
    xp_specs = [pl.BlockSpec((1, SUBLANE, D), lambda b, t, s=s: (b, jnp.clip((ns * t + s) * rb - 1, 0, nrb - 1), 0))
                for s in range(ns)]
    xn_specs = [pl.BlockSpec((1, SUBLANE, D), lambda b, t, s=s: (b, jnp.clip((ns * t + s + 1) * rb, 0, nrb - 1), 0))
                for s in range(ns)]
    in_specs = x_specs + xp_specs + xn_specs + [
        pl.BlockSpec((1, CTX_LEN, D), lambda b, t: (b, 0, 0)),
        mod, mod, full(shc), full(scc),
    ] + [full(a) for a in consts] + [tab, tab, tab, tab, full(tabs["gsum"])]
    nh = MLA_HEADS + GQA_HEADS
    head_major = pl.BlockSpec((1, nh, blk, LANE), lambda b, t: (b, 0, t, 0))
    out_w = [(_W_GDN, F32), (_W_BA, F32), (D_MIX, BF16)]
    out_specs = [head_major, head_major, pl.BlockSpec((1, nh * MLA_V, blk), lambda b, t: (b, 0, t))
                 ] + [tok(w) for w, _ in out_w]
    out_shape = [jax.ShapeDtypeStruct((B, nh, LT, LANE), BF16), jax.ShapeDtypeStruct((B, nh, LT, LANE), BF16),
                 jax.ShapeDtypeStruct((B, nh * MLA_V, LT), BF16)
                 ] + [jax.ShapeDtypeStruct((B, LT, w), dt) for w, dt in out_w]
    return pl.pallas_call(
        functools.partial(_inproj_kernel, nt_lat, ns),
        grid=(B, nt // ns),
        in_specs=in_specs,
        out_specs=out_specs,
        out_shape=out_shape,
        compiler_params=_params("parallel", "arbitrary"),
        name="inproj",
    )(*([x] * (3 * ns)), ctx, shl, scl, shc, scc, *consts,
      tabs["ca"], tabs["sa"], tabs["cb"], tabs["sb"], tabs["gsum"])


def _attn_kernel(q_ref, k_ref, vt_ref, o_ref, s0_scr, s1_scr, p0_scr, p1_scr):
    nh = q_ref.shape[1]
    tq = s0_scr.shape[1]
    n_items = nh * (q_ref.shape[2] // tq)
    assert n_items % 2 == 0
    dv = vt_ref.shape[1] // nh
    ones_rows = jnp.ones((2 * SUBLANE, k_ref.shape[2]), BF16)
    s_scr, p_scr = (s0_scr, s1_scr), (p0_scr, p1_scr)

    def scores(i, par):
        tile, h = i // nh, i % nh
        q = q_ref[0, h, pl.ds(pl.multiple_of(tile * tq, tq), tq), :]
        lk = k_ref.shape[2]
        half = lk // ATTN_QK_SPLIT if lk % (ATTN_QK_SPLIT * 2 * SUBLANE) == 0 else lk
        m = None
        for lo in range(0, lk, half):
            s = _dot_nt(k_ref[0, h, lo:lo + half, :], q)
            s_scr[par][lo:lo + half, :] = s
            mh = jnp.max(s, axis=0, keepdims=True)
            m = mh if m is None else jnp.maximum(m, mh)
        return m

    def probs(par, m):
        lk = k_ref.shape[2]
        half = lk // ATTN_EXP_SPLIT if lk % (ATTN_EXP_SPLIT * 2 * SUBLANE) == 0 else lk
        for lo in range(0, lk, half):
            p_scr[par][lo:lo + half, :] = jnp.exp2(s_scr[par][lo:lo + half, :] - m).astype(BF16)

    def values(i, par):
        tile, h = i // nh, i % nh
        r0 = pl.multiple_of(h * dv, dv)
        lhs = jnp.concatenate([vt_ref[0, pl.ds(r0, dv), :], ones_rows], axis=0)
        r = _dot(lhs, p_scr[par][...])
        o_ref[0, tile, pl.ds(r0, dv), :] = (r[0:dv] / r[dv:dv + 1]).astype(o_ref.dtype)

    def stage(i, par, m_i):
        values(i - 1, 1 - par)
        m_next = scores(i + 1, 1 - par)
        probs(par, m_i)
        return m_next

    m0 = scores(0, 0)
    m1 = scores(1, 1)
    probs(0, m0)

    def body(j, m_odd):
        i = 2 * j + 1
        m_even = stage(i, 1, m_odd)
        return stage(i + 1, 0, m_even)

    m_last = lax.fori_loop(0, (n_items - 2) // 2, body, m1)
    values(n_items - 2, 0)
    probs(1, m_last)
    values(n_items - 1, 1)


def _attention(q, k, v_t, q_start, q_len, k_start, k_len):
    B, nh = q.shape[0], q.shape[1]
    tq = TOK_TILE
    tiles = min(ATTN_TILES_PER_STEP, q_len // tq)
    qb = tiles * tq
    nq = q_len // qb
    q0 = q_start // qb
    kblk = k_start // k_len
    wo = v_t.shape[1]
    return pl.pallas_call(
        _attn_kernel,
        grid=(B, nq),
        in_specs=[pl.BlockSpec((1, nh, qb, LANE), lambda b, i: (b, 0, q0 + i, 0)),
                  pl.BlockSpec((1, nh, k_len, LANE), lambda b, i: (b, 0, kblk, 0)),
                  pl.BlockSpec((1, wo, k_len), lambda b, i: (b, 0, kblk))],
        out_specs=pl.BlockSpec((1, tiles, wo, tq), lambda b, i: (b, i, 0, 0)),
        out_shape=jax.ShapeDtypeStruct((B, q_len // tq, wo, tq), BF16),
        scratch_shapes=[pltpu.VMEM((k_len, tq), F32), pltpu.VMEM((k_len, tq), F32),
                        pltpu.VMEM((k_len, tq), BF16), pltpu.VMEM((k_len, tq), BF16)],
        compiler_params=_params("parallel", "arbitrary"),
        name="attention",
    )(q, k, v_t)


def _gdn_kernel(n_lat, n_all, qkv_ref, bg_ref, o_ref, mp_ref, qp_ref, cc_ref, gl_ref):
    C, W, U = GDN_CHUNK, GDN_WIDTH, GDN_UNROLL
    n_ctx = n_all - n_lat

    ri = lax.broadcasted_iota(jnp.int32, (C, W), 0)
    cj = lax.broadcasted_iota(jnp.int32, (C, W), 1) % C
    lane_head = lax.broadcasted_iota(jnp.int32, (C, W), 1) // GDN_DV
    diag = ri == cj
    eye_lb = jnp.where(diag, 1.0, 0.0).astype(F32)
    blk4 = jnp.where(ri // 4 == cj // 4, 1.0, 0.0).astype(F32)
    lvl_masks = [jnp.where(jnp.logical_and(ri // (2 * bs) == cj // (2 * bs), ri // bs != cj // bs), 1.0, 0.0
                           ).astype(F32) for bs in (4, 8, 16, 32)]
    r2 = lax.broadcasted_iota(jnp.int32, (C, C), 0)
    c2 = lax.broadcasted_iota(jnp.int32, (C, C), 1)
    ones8 = jnp.ones((SUBLANE, C), F32)
    rb = lax.broadcasted_iota(jnp.int32, (GDN_HEADS * C, W), 0) // C
    cbk = lax.broadcasted_iota(jnp.int32, (GDN_HEADS * C, W), 1) // C
    bmask = jnp.where(rb == cbk, 1.0, 0.0).astype(BF16)
    bmask_f = bmask.astype(F32)
    incl = [ri >= cj, ri <= cj]
    incl_t = [jnp.where(cj >= ri, 1.0, 0.0).astype(F32), jnp.where(cj <= ri, 1.0, 0.0).astype(F32)]
    tri = [jnp.where(r2 >= c2, 1.0, 0.0).astype(F32), jnp.where(r2 <= c2, 1.0, 0.0).astype(F32)]
    last_row = [C - 1, 0]

    def bd(y):
        yb = y.astype(BF16)
        return jnp.concatenate([yb] * GDN_HEADS, axis=0) * bmask

    def mm(x, ybd):
        return _dot(x.astype(BF16), ybd)

    def diag_blocks(full):
        acc = full[0:C]
        for h in range(1, GDN_HEADS):
            acc = acc + full[C * h:C * (h + 1)]
        return acc

    def expand(bg, col0):
        out = jnp.broadcast_to(bg[:, col0:col0 + 1], (C, W))
        for h in range(1, GDN_HEADS):
            out = jnp.where(lane_head == h, jnp.broadcast_to(bg[:, col0 + h:col0 + h + 1], (C, W)), out)
        return out

    def phase1(it, carry):
        probs = [(c, d) for c in range(U) for d in range(2)]
        each = lambda f, *ls: [f(*a) for a in zip(*ls)]
        per = lambda f: [f(c, d) for c, d in probs]
        rows = [pl.multiple_of((it * U + c) * C, C) for c in range(U)]
        q = [qkv_ref[0, pl.ds(r, C), 0:W] for r in rows]
        k = [qkv_ref[0, pl.ds(r, C), W:2 * W] for r in rows]
        v = [qkv_ref[0, pl.ds(r, C), 2 * W:3 * W] for r in rows]
        bg = [bg_ref[0, pl.ds(r, C), :] for r in rows]
        kbf = each(lambda x: x.astype(BF16), k)
        gram = each(lambda kb_, q_: _dot_nt(jnp.concatenate([kb_, q_.astype(BF16)], axis=0),
                                            jnp.concatenate([kb_] * GDN_HEADS, axis=0) * bmask), kbf, q)
        beta = per(lambda c, d: expand(bg[c], GDN_HEADS * d))
        g = per(lambda c, d: expand(bg[c], _N_BA + GDN_HEADS * d))
        dirs = [d for _, d in probs]
        chunk = [c for c, _ in probs]
        gc = each(lambda x, d: jnp.dot(tri[d], x, precision=HIGHEST, preferred_element_type=F32), g, dirs)
        g2 = each(lambda x, d: jnp.dot(ones8, x * incl_t[d], precision=HIGHEST, preferred_element_type=F32), g, dirs)
        dlt = each(lambda a, b: a - jnp.broadcast_to(b[0:1], (C, W)), gc, g2)
        dec = each(lambda x, d: jnp.where(incl[d], jnp.exp(jnp.where(incl[d], x, 0.0)), 0.0), dlt, dirs)
        gcl = each(lambda x, d: x[last_row[d]:last_row[d] + 1], gc, dirs)
        egc = each(jnp.exp, gc)
        a_mat = each(lambda c, b, de: gram[c][0:C] * b * jnp.where(diag, 0.0, de), chunk, beta, dec)
        a_intra = each(lambda c, de: (gram[c][C:2 * C] * de).astype(BF16), chunk, dec)

        b_0 = each(lambda a: -(a * blk4), a_mat)
        p_0 = each(lambda b: b + mm(b, bd(b)), b_0)
        t_mat = each(lambda b, p: eye_lb + b + mm(p, bd(b)), b_0, p_0)
        for msk in lvl_masks:
            y = each(lambda a, x: mm(a * msk, bd(x)), a_mat, t_mat)
            t_mat = each(lambda x, y_: x - mm(x, bd(y_)), t_mat, y)

        u = each(lambda t, c, b: mm(t, bd(v[c] * b)), t_mat, chunk, beta)
        w = each(lambda t, c, b, e: mm(t, bd(k[c] * b * e)), t_mat, chunk, beta, egc)
        kd_t = each(lambda c, gl_, gc_: jnp.transpose(k[c] * jnp.exp(gl_ - gc_)).astype(BF16), chunk, gcl, gc)
        kwu = each(lambda kt, w_, u_: _dot(kt, jnp.concatenate([w_, u_], axis=1).astype(BF16)), kd_t, w, u)
        aw = each(lambda a, w_: _dot(a, bd(w_)), a_intra, w)
        au = each(lambda a, u_: _dot(a, bd(u_)), a_intra, u)
        for p, (c, d) in enumerate(probs):
            n = it * U + c
            mp_ref[d, n] = (-diag_blocks(kwu[p][:, 0:W] * bmask_f)).astype(BF16)
            cc_ref[d, n] = diag_blocks(kwu[p][:, W:2 * W] * bmask_f)
            qp_ref[d, n] = (q[c] * egc[p] - aw[p]).astype(BF16)
            gl_ref[d, n] = jnp.broadcast_to(jnp.exp(gcl[p]), (SUBLANE, W))
        for c in range(U):
            o_ref[0, pl.ds(rows[c], C), :] = au[2 * c] + au[2 * c + 1]
        return carry

    assert n_all % U == 0
    lax.fori_loop(0, n_all // U, phase1, 0)

    def phase2(i, carry):
        n = (jnp.where(i < n_ctx, n_lat + i, i - n_ctx), n_all - 1 - i)
        r = [_dot(jnp.concatenate([mp_ref[d, n[d]], qp_ref[d, n[d]]], axis=0), bd(carry[d])) for d in range(2)]
        new = []
        for d in range(2):
            r0 = pl.multiple_of(n[d] * C, C)
            o_ref[0, pl.ds(r0, C), :] += r[d][C:2 * C]
            new.append(gl_ref[d, n[d]][0:1] * carry[d] + r[d][0:C] + cc_ref[d, n[d]])
        return tuple(new)

    zero = jnp.zeros((C, W), F32)
    lax.fori_loop(0, n_all, phase2, (zero, zero))


def _gdn(qkv, bg, n_lat_tokens):
    B, LT, _ = qkv.shape
    n_all = LT // GDN_CHUNK
    n_lat = n_lat_tokens // GDN_CHUNK
    W = GDN_WIDTH
    return pl.pallas_call(
        functools.partial(_gdn_kernel, n_lat, n_all),
        grid=(B,),
        in_specs=[
            pl.BlockSpec((1, LT, 3 * W), lambda b: (b, 0, 0)),
            pl.BlockSpec((1, LT, LANE), lambda b: (b, 0, 0)),
        ],
        out_specs=pl.BlockSpec((1, LT, W), lambda b: (b, 0, 0)),
        out_shape=jax.ShapeDtypeStruct((B, LT, W), F32),
        scratch_shapes=[
            pltpu.VMEM((2, n_all, GDN_CHUNK, W), BF16),
            pltpu.VMEM((2, n_all, GDN_CHUNK, W), BF16),
            pltpu.VMEM((2, n_all, GDN_CHUNK, W), F32),
            pltpu.VMEM((2, n_all, SUBLANE, W), F32),
        ],
        compiler_params=_params("parallel"),
        name="gdn",
    )(qkv, bg)


def _merge_kernel(final, x_ref, oa_ref, og_ref, sg_ref, gt_ref, ong_ref, gsum_ref, wout_ref, fg_ref, o_ref):
    wa = oa_ref.shape[2]
    subs = [slice(TOK_TILE * s, TOK_TILE * (s + 1)) for s in range(oa_ref.shape[1])]
    ogs = [og_ref[0, sl, :] for sl in subs]
    ssq = [_group_sum(og * og, gsum_ref[...]) for og in ogs]
    ms = []
    for s, sl in enumerate(subs):
        ogn = ogs[s] * lax.rsqrt(ssq[s] * (1.0 / GDN_DV) + EPS) * ong_ref[...]
        ma = (jnp.transpose(oa_ref[0, s].astype(F32)) * sg_ref[0, sl, 0:wa].astype(F32)).astype(BF16)
        mg = (ogn * sg_ref[0, sl, wa:].astype(F32)).astype(BF16)
        ms.append((ma, mg))
    ys = [_dot(ma, wout_ref[0:wa, :]) + _dot(mg, wout_ref[wa:, :]) for ma, mg in ms]
    for s, sl in enumerate(subs):
        xn = x_ref[0, sl, :] + gt_ref[0] * ys[s]
        if final:
            xn = xn * lax.rsqrt(jnp.mean(xn * xn, axis=-1, keepdims=True) + EPS) * fg_ref[...]
        o_ref[0, sl, :] = xn


def _merge(x, o_attn, o_gdn, sg, gt, lw, gsum, final_g, row_start, final):
    B, L, D = x.shape
    sub = min(MERGE_SUB, L // TOK_TILE)
    rows = sub * TOK_TILE
    nt = L // rows
    t0 = row_start // rows
    full = lambda a: pl.BlockSpec(a.shape, lambda b, t: (0,) * a.ndim)
    return pl.pallas_call(
        functools.partial(_merge_kernel, final),
        grid=(B, nt),
        in_specs=[
            pl.BlockSpec((1, rows, D), lambda b, t: (b, t, 0)),
            pl.BlockSpec((1, sub, o_attn.shape[2], TOK_TILE), lambda b, t: (b, t, 0, 0)),
            pl.BlockSpec((1, rows, GDN_WIDTH), lambda b, t: (b, t0 + t, 0)),
            pl.BlockSpec((1, rows, D_MIX), lambda b, t: (b, t0 + t, 0)),
            pl.BlockSpec((1, 1, D), lambda b, t: (b, 0, 0)),
            full(lw["on_g"]), full(gsum), full(lw["w_out"]), full(final_g),
        ],
        out_specs=pl.BlockSpec((1, rows, D), lambda b, t: (b, t, 0)),
        out_shape=jax.ShapeDtypeStruct((B, L, D), F32),
        compiler_params=_params("parallel", "arbitrary"),
        name="merge",
    )(x, o_attn, o_gdn, sg, gt, lw["on_g"], gsum, lw["w_out"], final_g)


def _layer_weights(l, w_in, norm_g, mla_qn_g, mla_w_uq, mla_kvn_g, mla_w_ukv, gqa_qn_g, gqa_kn_g,
                   gdn_conv_w, gdn_a_log, gdn_dt_bias, gdn_on_g, w_out):
    D = D_MODEL
    w = w_in[l]
    splits = np.cumsum([0, MLA_Q_RANK, MLA_KV_RANK, MLA_ROPE, GQA_HEADS * GQA_DIM, GQA_KV_HEADS * GQA_DIM,
                        GQA_KV_HEADS * GQA_DIM, 3 * GDN_HEADS * GDN_DK, 2 * GDN_HEADS, 2 * GDN_HEADS, D_MIX])
    seg = [w[:, int(a):int(b)] for a, b in zip(splits[:-1], splits[1:])]
    cq, ckv, kr, gq, gk, gv, gdn, bb, aa, gate = seg
    z = lambda n: jnp.zeros((D, n), w.dtype)
    kr_p = jnp.concatenate([z(MLA_NOPE), kr, z(LANE - MLA_NOPE - MLA_ROPE)], axis=1)
    k0, k1 = gk[:, :GQA_DIM], gk[:, GQA_DIM:]
    gk2 = jnp.concatenate([k0, k1, k1, k0], axis=1)
    ba = jnp.concatenate([bb, aa, z(LANE - 2 * _N_BA)], axis=1)
    w_p = jnp.concatenate([cq, ckv, kr_p, gq, gk2, gdn, ba, gate], axis=1).astype(BF16)

    dqk = MLA_NOPE + MLA_ROPE
    uq = mla_w_uq[l].reshape(MLA_Q_RANK, MLA_HEADS, dqk)
    uq_p = jnp.concatenate([uq, jnp.zeros((MLA_Q_RANK, MLA_HEADS, LANE - dqk), uq.dtype)], axis=2)
    ukv = mla_w_ukv[l].reshape(MLA_KV_RANK, MLA_HEADS, MLA_NOPE + MLA_V)
    uk_p = jnp.concatenate([ukv[:, :, :MLA_NOPE],
                            jnp.zeros((MLA_KV_RANK, MLA_HEADS, LANE - MLA_NOPE), ukv.dtype)], axis=2)
    uv = ukv[:, :, MLA_NOPE:]
    return {
        "norm_g": norm_g[l].reshape(1, D),
        "w_in": w_p,
        "qn_g": mla_qn_g[l].reshape(1, -1),
        "w_uq": uq_p.reshape(MLA_Q_RANK, MLA_HEADS * LANE).astype(BF16),
        "kvn_g": mla_kvn_g[l].reshape(1, -1),
        "w_uk": uk_p.reshape(MLA_KV_RANK, MLA_HEADS * LANE).astype(BF16),
        "w_uv_t": uv.reshape(MLA_KV_RANK, MLA_HEADS * MLA_V).T.astype(BF16),
        "w_gv_t": gv.T.astype(BF16),
        "gq_g": (jnp.tile(gqa_qn_g[l], GQA_HEADS) * (GQA_SCALE * LOG2E)).reshape(1, -1),
        "gk_g": jnp.tile(gqa_kn_g[l], 2 * 2).reshape(1, -1),
        "conv_w": jnp.concatenate([gdn_conv_w[l], jnp.zeros((SUBLANE - GDN_CONV, _W_GDN), F32)], axis=0),
        "alog_x": jnp.concatenate([jnp.zeros((_N_BA,), F32), gdn_a_log[l].reshape(-1),
                                   jnp.zeros((LANE - 2 * _N_BA,), F32)]).reshape(1, LANE),
        "dtb_x": jnp.concatenate([jnp.zeros((_N_BA,), F32), gdn_dt_bias[l].reshape(-1),
                                  jnp.zeros((LANE - 2 * _N_BA,), F32)]).reshape(1, LANE),
        "on_g": jnp.tile(gdn_on_g[l], GDN_HEADS).reshape(1, -1),
        "w_out": w_out[l].astype(BF16),
    }


def _rope_tables(seq_len):
    LT = seq_len + CTX_LEN
    t = np.arange(seq_len)
    rows, cols = (t // GRID_W).astype(np.float64), (t % GRID_W).astype(np.float64)

    def fill(c, s, lane0, d, pos):
        half = d // 2
        inv = ROPE_BASE ** (-np.arange(0, d, 2, dtype=np.float64) / d)
        ang = pos[:, None] * inv[None, :]
        c[:seq_len, lane0:lane0 + half] = np.cos(ang)
        c[:seq_len, lane0 + half:lane0 + d] = np.cos(ang)
        s[:seq_len, lane0:lane0 + half] = -np.sin(ang)
        s[:seq_len, lane0 + half:lane0 + d] = np.sin(ang)

    ca, sa = np.ones((LT, LANE)), np.zeros((LT, LANE))
    fill(ca, sa, MLA_NOPE, MLA_ROPE // 2, rows)
    fill(ca, sa, MLA_NOPE + MLA_ROPE // 2, MLA_ROPE // 2, cols)
    cb, sb = np.ones((LT, LANE)), np.zeros((LT, LANE))
    for h0 in (0, GQA_DIM):
        fill(cb, sb, h0, GQA_DIM // 2, rows)
        fill(cb, sb, h0 + GQA_DIM // 2, GQA_DIM // 2, cols)
    lane = np.arange(3 * LANE)
    gsum = (lane[:, None] // GQA_DIM == lane[None, :] // GQA_DIM).astype(np.float32)
    return {"ca": jnp.asarray(ca, F32), "sa": jnp.asarray(sa, F32),
            "cb": jnp.asarray(cb, F32), "sb": jnp.asarray(sb, F32),
            "gsum": jnp.asarray(gsum, BF16)}


def kernel(x, c, ctx, c_ctx, w_ada, b_ada, norm_g, w_in, mla_qn_g, mla_w_uq, mla_kvn_g, mla_w_ukv,
           gqa_qn_g, gqa_kn_g, gdn_conv_w, gdn_a_log, gdn_dt_bias, gdn_on_g, w_out, final_g):
    B, L, D = x.shape
    assert D == D_MODEL and L % TOK_TILE == 0 and ctx.shape[1] == CTX_LEN
    tabs = _rope_tables(L)
    rows = -(-(B + 1) // SUBLANE) * SUBLANE
    cvec = jnp.concatenate([c, c_ctx[None, :], jnp.zeros((rows - B - 1, D), F32)], axis=0)
    mod = _adaln(cvec, w_ada, b_ada)
    fg = final_g.reshape(1, D)
    gs256 = tabs["gsum"][:GDN_WIDTH, :GDN_WIDTH]

    for l in range(DEPTH):
        lw = _layer_weights(l, w_in, norm_g, mla_qn_g, mla_w_uq, mla_kvn_g, mla_w_ukv, gqa_qn_g,
                            gqa_kn_g, gdn_conv_w, gdn_a_log, gdn_dt_bias, gdn_on_g, w_out)
        ml = mod[l, :B].reshape(B, 1, 3 * D)
        mc = mod[l, B].reshape(1, 3 * D)
        shl, scl, gtl = ml[..., :D], ml[..., D:2 * D], ml[..., 2 * D:]
        shc, scc, gtc = mc[:, :D], mc[:, D:2 * D], mc[:, 2 * D:]
        q, k, v_t, qkv, bg, sg = _inproj(x, ctx, shl, scl, shc, scc, lw, tabs)
        o_gdn = _gdn(qkv, bg, L)
        o_lat = _attention(q, k, v_t, 0, L, 0, L + CTX_LEN)
        last = l == DEPTH - 1
        x_new = _merge(x, o_lat, o_gdn, sg, gtl, lw, gs256, fg, 0, last)
        if not last:
            o_ctx = _attention(q, k, v_t, L, CTX_LEN, L, CTX_LEN)
            gtc_b = jnp.broadcast_to(gtc.reshape(1, 1, D), (B, 1, D))
            ctx = _merge(ctx, o_ctx, o_gdn, sg, gtc_b, lw, gs256, fg, L, False)
        x = x_new
    return x
```

```python
import functools
import math

import numpy as np
import jax
import jax.numpy as jnp
from jax import lax
from jax.experimental import pallas as pl
from jax.experimental.pallas import tpu as pltpu

F32 = jnp.float32
BF16 = jnp.bfloat16
HIGHEST = lax.Precision.HIGHEST

D_MODEL = 1024
DEPTH = 2
CTX_LEN = 256
GRID_W = 64
D_MIX = 1024
EPS = 1e-6
ROPE_BASE = 10000.0

MLA_HEADS = 6
MLA_NOPE = 64
MLA_ROPE = 32
MLA_V = 64
MLA_Q_RANK = 384
MLA_KV_RANK = 256
LOG2E = math.log2(math.e)
MLA_SCALE = (MLA_NOPE + MLA_ROPE) ** -0.5

GQA_HEADS = 6
GQA_KV_HEADS = 2
GQA_DIM = 64
GQA_SCALE = GQA_DIM ** -0.5

GDN_HEADS = 4
GDN_DK = 64
GDN_DV = 64
GDN_WIDTH = GDN_HEADS * GDN_DV
GDN_CONV = 5
GDN_CHUNK = 64

LANE = 128
SUBLANE = 8
TOK_TILE = 256
ATTN_EXP_SPLIT = 2
ATTN_QK_SPLIT = 2
ATTN_TILES_PER_STEP = 4
INPROJ_SUB = 3
MERGE_SUB = 4
GDN_UNROLL = 6
VMEM_LIMIT = 48 * 1024 * 1024

_W_CQ = MLA_Q_RANK
_W_CKV = MLA_KV_RANK
_W_KR = LANE
_W_GQ = GQA_HEADS * GQA_DIM
_W_GK2 = 2 * LANE
_W_GDN = 3 * GDN_WIDTH
_W_BA = LANE
_N_BA = 2 * GDN_HEADS
_W_GATE = D_MIX
_OFFS = np.cumsum([0, _W_CQ, _W_CKV, _W_KR, _W_GQ, _W_GK2, _W_GDN, _W_BA, _W_GATE])
(_O_CQ, _O_CKV, _O_KR, _O_GQ, _O_GK2, _O_GDN, _O_BA, _O_GATE, D_IN_P) = [int(v) for v in _OFFS]


def _dot(a, b):
    return jnp.dot(a, b, preferred_element_type=F32)


def _dot_nt(a, b):
    return lax.dot_general(a, b, (((1,), (1,)), ((), ())), preferred_element_type=F32)


def _params(*sem):
    return pltpu.CompilerParams(dimension_semantics=sem, vmem_limit_bytes=VMEM_LIMIT)


def _adaln_kernel(c_ref, w_ref, b_ref, o_ref):
    c = c_ref[...]
    s = c * jax.nn.sigmoid(c)
    o_ref[0] = jnp.dot(s, w_ref[0], precision=HIGHEST, preferred_element_type=F32) + b_ref[0]


def _adaln(cvec, w_ada, b_ada):
    rows = cvec.shape[0]
    nblk = w_ada.shape[2] // D_MODEL
    return pl.pallas_call(
        _adaln_kernel,
        grid=(DEPTH, nblk),
        in_specs=[
            pl.BlockSpec((rows, D_MODEL), lambda l, j: (0, 0)),
            pl.BlockSpec((1, D_MODEL, D_MODEL), lambda l, j: (l, 0, j)),
            pl.BlockSpec((1, 1, D_MODEL), lambda l, j: (l, 0, j)),
        ],
        out_specs=pl.BlockSpec((1, rows, D_MODEL), lambda l, j: (l, 0, j)),
        out_shape=jax.ShapeDtypeStruct((DEPTH, rows, 3 * D_MODEL), F32),
        compiler_params=_params("arbitrary", "arbitrary"),
        name="adaln",
    )(cvec, w_ada, b_ada.reshape(DEPTH, 1, 3 * D_MODEL))


def _group_sum(sq, gsum):
    return _dot(sq.astype(BF16), gsum)


def _rope_block(xb, c, s, half):
    lane = lax.broadcasted_iota(jnp.int32, xb.shape, 1)
    up = pltpu.roll(xb, LANE - half, 1)
    dn = pltpu.roll(xb, half, 1)
    partner = jnp.where((lane % (2 * half)) < half, up, dn)
    return xb * c + partner * s


def _inproj_kernel(nt_lat, ns, *refs):
    x_refs, xp_refs, xn_refs = refs[0:ns], refs[ns:2 * ns], refs[2 * ns:3 * ns]
    (ctx_ref, shl_ref, scl_ref, shc_ref, scc_ref, ng_ref, win_ref, qng_ref, wuq_ref, kvng_ref, wuk_ref,
     wuvt_ref, wgvt_ref, gqg_ref, gkg_ref, cw_ref, alog_ref, dtb_ref, ca_ref, sa_ref, cb_ref, sb_ref,
     gsum_ref, q_ref, k_ref, vt_ref, qkv_ref, bg_ref, sg_ref) = refs[3 * ns:]
    t = pl.program_id(1)
    tm = TOK_TILE
    nv = MLA_HEADS * MLA_V
    gsum = gsum_ref[...]
    lane = lax.broadcasted_iota(jnp.int32, (tm, LANE), 1)
    row = lax.broadcasted_iota(jnp.int32, (tm + 2 * SUBLANE, 1), 0)
    tiles = [ns * t + s for s in range(ns)]
    rows = [slice(tm * s, tm * (s + 1)) for s in range(ns)]

    hbe, hb = [], []
    for s in range(ns):
        is_ctx = tiles[s] == nt_lat
        xt = jnp.where(is_ctx, ctx_ref[0], x_refs[s][0])
        xe = jnp.concatenate([xp_refs[s][0], xt, xn_refs[s][0]], axis=0)
        sh = jnp.where(is_ctx, shc_ref[...], shl_ref[0])
        sc = jnp.where(is_ctx, scc_ref[...], scl_ref[0])
        ms = jnp.mean(xe * xe, axis=-1, keepdims=True)
        he = (xe * lax.rsqrt(ms + EPS) * ng_ref[...]) * (1.0 + sc) + sh
        hbe.append(he.astype(BF16))
        hb.append(he[SUBLANE:SUBLANE + tm].astype(BF16))

    z = []
    for s in range(ns):
        zs = dict(
            g=_dot(hbe[s], win_ref[:, _O_GDN:_O_GDN + _W_GDN]),
            cq=_dot(hb[s], win_ref[:, _O_CQ:_O_CQ + _W_CQ]),
            ckv=_dot(hb[s], win_ref[:, _O_CKV:_O_CKV + _W_CKV]),
            q=_dot(hb[s], win_ref[:, _O_GQ:_O_GQ + _W_GQ]),
            k=_dot(hb[s], win_ref[:, _O_GK2:_O_GK2 + _W_GK2]),
            kr=_dot(hb[s], win_ref[:, _O_KR:_O_KR + _W_KR]),
            ba=_dot(hb[s], win_ref[:, _O_BA:_O_BA + _W_BA]))
        vg = _dot_nt(wgvt_ref[...], hb[s]).astype(BF16)
        for j in range(GQA_HEADS):
            kv = j // (GQA_HEADS // GQA_KV_HEADS)
            vt_ref[0, nv + GQA_DIM * j:nv + GQA_DIM * (j + 1), rows[s]] = vg[GQA_DIM * kv:GQA_DIM * (kv + 1)]
        zs["gate"] = _dot(hb[s], win_ref[:, _O_GATE:_O_GATE + _W_GATE])
        z.append(zs)

    for s in range(ns):
        zs, rs = z[s], rows[s]
        is_ctx = tiles[s] == nt_lat
        ca, sa = ca_ref[rs, :], sa_ref[rs, :]
        cb, sb = cb_ref[rs, :], sb_ref[rs, :]

        has_prev = jnp.logical_and(tiles[s] > 0, jnp.logical_not(is_ctx))
        has_next = tiles[s] < nt_lat - 1
        keep = jnp.logical_and(jnp.logical_or(row >= SUBLANE, has_prev),
                               jnp.logical_or(row < SUBLANE + tm, has_next))
        zg = jnp.where(keep, zs["g"], 0.0)
        pad = GDN_CONV // 2
        y = jnp.zeros((tm, _W_GDN), F32)
        for j in range(GDN_CONV):
            o = SUBLANE - pad + j
            y = y + zg[o:o + tm] * cw_ref[j:j + 1, :]
        y = y * jax.nn.sigmoid(y)
        gs = gsum[:GDN_WIDTH, :GDN_WIDTH]
        qg = y[:, :GDN_WIDTH]
        kg = y[:, GDN_WIDTH:2 * GDN_WIDTH]
        qkv_ref[0, rs, 0:GDN_WIDTH] = qg * lax.rsqrt(_group_sum(qg * qg, gs) + EPS) * (GDN_DK ** -0.5)
        qkv_ref[0, rs, GDN_WIDTH:2 * GDN_WIDTH] = kg * lax.rsqrt(_group_sum(kg * kg, gs) + EPS)
        qkv_ref[0, rs, 2 * GDN_WIDTH:] = y[:, 2 * GDN_WIDTH:]
        za = zs["ba"] + dtb_ref[...]
        softplus = jnp.maximum(za, 0.0) + jnp.log(1.0 + jnp.exp(-jnp.abs(za)))
        bg_ref[0, rs, :] = jnp.where(lane < _N_BA, jax.nn.sigmoid(zs["ba"]), -jnp.exp(alog_ref[...]) * softplus)

        zcq = zs["cq"]
        cqn = zcq * lax.rsqrt(jnp.mean(zcq * zcq, axis=-1, keepdims=True) + EPS) * qng_ref[...]
        q = _dot(cqn.astype(BF16), wuq_ref[...]) * (MLA_SCALE * LOG2E)
        for h in range(MLA_HEADS):
            sl = slice(LANE * h, LANE * (h + 1))
            q_ref[0, h, rs, :] = _rope_block(q[:, sl], ca, sa, MLA_ROPE // 4).astype(BF16)

        zckv = zs["ckv"]
        ckvn = (zckv * lax.rsqrt(jnp.mean(zckv * zckv, axis=-1, keepdims=True) + EPS) * kvng_ref[...]).astype(BF16)
        kr = _rope_block(zs["kr"], ca, sa, MLA_ROPE // 4)
        kn = _dot(ckvn, wuk_ref[...])
        for h in range(MLA_HEADS):
            sl = slice(LANE * h, LANE * (h + 1))
            k_ref[0, h, rs, :] = (kn[:, sl] + kr).astype(BF16)
        vt_ref[0, 0:nv, rs] = _dot_nt(wuvt_ref[...], ckvn).astype(BF16)

        zq, zk = zs["q"], zs["k"]
        qn = zq * lax.rsqrt(_group_sum(zq * zq, gsum) * (1.0 / GQA_DIM) + EPS) * gqg_ref[...]
        for p in range(GQA_HEADS // 2):
            blk = _rope_block(qn[:, LANE * p:LANE * (p + 1)], cb, sb, GQA_DIM // 4)
            q_ref[0, MLA_HEADS + 2 * p, rs, :] = jnp.where(lane < GQA_DIM, blk, 0.0).astype(BF16)
            q_ref[0, MLA_HEADS + 2 * p + 1, rs, :] = jnp.where(lane >= GQA_DIM, blk, 0.0).astype(BF16)
        kn2 = zk * lax.rsqrt(_group_sum(zk * zk, gsum[:_W_GK2, :_W_GK2]) * (1.0 / GQA_DIM) + EPS) * gkg_ref[...]
        kblks = [_rope_block(kn2[:, LANE * p:LANE * (p + 1)], cb, sb, GQA_DIM // 4).astype(BF16) for p in range(2)]
        for j in range(GQA_HEADS):
            kv = j // (GQA_HEADS // GQA_KV_HEADS)
            k_ref[0, MLA_HEADS + j, rs, :] = kblks[0 if (j % 2) == kv else 1]

        zgate = zs["gate"]
        sg_ref[0, rs, :] = (zgate * jax.nn.sigmoid(zgate)).astype(BF16)


def _inproj(x, ctx, shl, scl, shc, scc, lw, tabs):
    B, L, D = x.shape
    ns = INPROJ_SUB
    nt_lat = L // TOK_TILE
    nt = nt_lat + CTX_LEN // TOK_TILE
    assert nt % ns == 0
    LT = L + CTX_LEN
    rb = TOK_TILE // SUBLANE
    nrb = L // SUBLANE
    blk = ns * TOK_TILE
    full = lambda a: pl.BlockSpec(a.shape, lambda b, t: (0,) * a.ndim)
    tok = lambda w: pl.BlockSpec((1, blk, w), lambda b, t: (b, t, 0))
    tab = pl.BlockSpec((blk, LANE), lambda b, t: (t, 0))
    mod = pl.BlockSpec((1, 1, D), lambda b, t: (b, 0, 0))
    consts = [lw["norm_g"], lw["w_in"], lw["qn_g"], lw["w_uq"], lw["kvn_g"], lw["w_uk"], lw["w_uv_t"], lw["w_gv_t"],
              lw["gq_g"], lw["gk_g"], lw["conv_w"], lw["alog_x"], lw["dtb_x"]]
    x_specs = [pl.BlockSpec((1, TOK_TILE, D), lambda b, t, s=s: (b, jnp.minimum(ns * t + s, nt_lat - 1), 0))
               for s in range(ns)]
    xp_specs = [pl.BlockSpec((1, SUBLANE, D), lambda b, t, s=s: (b, jnp.clip((ns * t + s) * rb - 1, 0, nrb - 1), 0))
                for s in range(ns)]
    xn_specs = [pl.BlockSpec((1, SUBLANE, D), lambda b, t, s=s: (b, jnp.clip((ns * t + s + 1) * rb, 0, nrb - 1), 0))
                for s in range(ns)]
    in_specs = x_specs + xp_specs + xn_specs + [
        pl.BlockSpec((1, CTX_LEN, D), lambda b, t: (b, 0, 0)),
        mod, mod, full(shc), full(scc),
    ] + [full(a) for a in consts] + [tab, tab, tab, tab, full(tabs["gsum"])]
    nh = MLA_HEADS + GQA_HEADS
    head_major = pl.BlockSpec((1, nh, blk, LANE), lambda b, t: (b, 0, t, 0))
    out_w = [(_W_GDN, F32), (_W_BA, F32), (D_MIX, BF16)]
    out_specs = [head_major, head_major, pl.BlockSpec((1, nh * MLA_V, blk), lambda b, t: (b, 0, t))
                 ] + [tok(w) for w, _ in out_w]
    out_shape = [jax.ShapeDtypeStruct((B, nh, LT, LANE), BF16), jax.ShapeDtypeStruct((B, nh, LT, LANE), BF16),
                 jax.ShapeDtypeStruct((B, nh * MLA_V, LT), BF16)
                 ] + [jax.ShapeDtypeStruct((B, LT, w), dt) for w, dt in out_w]
    return pl.pallas_call(
        functools.partial(_inproj_kernel, nt_lat, ns),
        grid=(B, nt // ns),
        in_specs=in_specs,
        out_specs=out_specs,
        out_shape=out_shape,
        compiler_params=_params("parallel", "arbitrary"),
        name="inproj",
    )(*([x] * (3 * ns)), ctx, shl, scl, shc, scc, *consts,
      tabs["ca"], tabs["sa"], tabs["cb"], tabs["sb"], tabs["gsum"])


def _attn_kernel(q_ref, k_ref, vt_ref, o_ref, s0_scr, s1_scr, p0_scr, p1_scr):
    nh = q_ref.shape[1]
    tq = s0_scr.shape[1]
    n_items = nh * (q_ref.shape[2] // tq)
    assert n_items % 2 == 0
    dv = vt_ref.shape[1] // nh
    ones_rows = jnp.ones((2 * SUBLANE, k_ref.shape[2]), BF16)
    s_scr, p_scr = (s0_scr, s1_scr), (p0_scr, p1_scr)

    def scores(i, par):
        tile, h = i // nh, i % nh
        q = q_ref[0, h, pl.ds(pl.multiple_of(tile * tq, tq), tq), :]
        lk = k_ref.shape[2]
        half = lk // ATTN_QK_SPLIT if lk % (ATTN_QK_SPLIT * 2 * SUBLANE) == 0 else lk
        m = None
        for lo in range(0, lk, half):
            s = _dot_nt(k_ref[0, h, lo:lo + half, :], q)
            s_scr[par][lo:lo + half, :] = s
            mh = jnp.max(s, axis=0, keepdims=True)
            m = mh if m is None else jnp.maximum(m, mh)
        return m

    def probs(par, m):
        lk = k_ref.shape[2]
        half = lk // ATTN_EXP_SPLIT if lk % (ATTN_EXP_SPLIT * 2 * SUBLANE) == 0 else lk
        for lo in range(0, lk, half):
            p_scr[par][lo:lo + half, :] = jnp.exp2(s_scr[par][lo:lo + half, :] - m).astype(BF16)

    def values(i, par):
        tile, h = i // nh, i % nh
        r0 = pl.multiple_of(h * dv, dv)
        lhs = jnp.concatenate([vt_ref[0, pl.ds(r0, dv), :], ones_rows], axis=0)
        r = _dot(lhs, p_scr[par][...])
        o_ref[0, tile, pl.ds(r0, dv), :] = (r[0:dv] / r[dv:dv + 1]).astype(o_ref.dtype)

    def stage(i, par, m_i):
        values(i - 1, 1 - par)
        m_next = scores(i + 1, 1 - par)
        probs(par, m_i)
        return m_next

    m0 = scores(0, 0)
    m1 = scores(1, 1)
    probs(0, m0)

    def body(j, m_odd):
        i = 2 * j + 1
        m_even = stage(i, 1, m_odd)
        return stage(i + 1, 0, m_even)

    m_last = lax.fori_loop(0, (n_items - 2) // 2, body, m1)
    values(n_items - 2, 0)
    probs(1, m_last)
    values(n_items - 1, 1)


def _attention(q, k, v_t, q_start, q_len, k_start, k_len):
    B, nh = q.shape[0], q.shape[1]
    tq = TOK_TILE
    tiles = min(ATTN_TILES_PER_STEP, q_len // tq)
    qb = tiles * tq
    nq = q_len // qb
    q0 = q_start // qb
    kblk = k_start // k_len
    wo = v_t.shape[1]
    return pl.pallas_call(
        _attn_kernel,
        grid=(B, nq),
        in_specs=[pl.BlockSpec((1, nh, qb, LANE), lambda b, i: (b, 0, q0 + i, 0)),
                  pl.BlockSpec((1, nh, k_len, LANE), lambda b, i: (b, 0, kblk, 0)),
                  pl.BlockSpec((1, wo, k_len), lambda b, i: (b, 0, kblk))],
        out_specs=pl.BlockSpec((1, tiles, wo, tq), lambda b, i: (b, i, 0, 0)),
        out_shape=jax.ShapeDtypeStruct((B, q_len // tq, wo, tq), BF16),
        scratch_shapes=[pltpu.VMEM((k_len, tq), F32), pltpu.VMEM((k_len, tq), F32),
                        pltpu.VMEM((k_len, tq), BF16), pltpu.VMEM((k_len, tq), BF16)],
        compiler_params=_params("parallel", "arbitrary"),
        name="attention",
    )(q, k, v_t)


def _gdn_kernel(n_lat, n_all, qkv_ref, bg_ref, o_ref, mp_ref, qp_ref, cc_ref, gl_ref):
    C, W, U = GDN_CHUNK, GDN_WIDTH, GDN_UNROLL
    n_ctx = n_all - n_lat

    ri = lax.broadcasted_iota(jnp.int32, (C, W), 0)
    cj = lax.broadcasted_iota(jnp.int32, (C, W), 1) % C
    lane_head = lax.broadcasted_iota(jnp.int32, (C, W), 1) // GDN_DV
    diag = ri == cj
    eye_lb = jnp.where(diag, 1.0, 0.0).astype(F32)
    blk4 = jnp.where(ri // 4 == cj // 4, 1.0, 0.0).astype(F32)
    lvl_masks = [jnp.where(jnp.logical_and(ri // (2 * bs) == cj // (2 * bs), ri // bs != cj // bs), 1.0, 0.0
                           ).astype(F32) for bs in (4, 8, 16, 32)]
    r2 = lax.broadcasted_iota(jnp.int32, (C, C), 0)
    c2 = lax.broadcasted_iota(jnp.int32, (C, C), 1)
    ones8 = jnp.ones((SUBLANE, C), F32)
    rb = lax.broadcasted_iota(jnp.int32, (GDN_HEADS * C, W), 0) // C
    cbk = lax.broadcasted_iota(jnp.int32, (GDN_HEADS * C, W), 1) // C
    bmask = jnp.where(rb == cbk, 1.0, 0.0).astype(BF16)
    bmask_f = bmask.astype(F32)
    incl = [ri >= cj, ri <= cj]
    incl_t = [jnp.where(cj >= ri, 1.0, 0.0).astype(F32), jnp.where(cj <= ri, 1.0, 0.0).astype(F32)]
    tri = [jnp.where(r2 >= c2, 1.0, 0.0).astype(F32), jnp.where(r2 <= c2, 1.0, 0.0).astype(F32)]
    last_row = [C - 1, 0]

    def bd(y):
        yb = y.astype(BF16)
        return jnp.concatenate([yb] * GDN_HEADS, axis=0) * bmask

    def mm(x, ybd):
        return _dot(x.astype(BF16), ybd)

    def exact01(m01, x):
        mb = m01.astype(BF16)
        hi = x.astype(BF16)
        r1 = x - hi.astype(F32)
        mid = r1.astype(BF16)
        lo = (r1 - mid.astype(F32)).astype(BF16)
        return _dot(mb, hi) + _dot(mb, mid) + _dot(mb, lo)

    def diag_blocks(full):
        acc = full[0:C]
        for h in range(1, GDN_HEADS):
            acc = acc + full[C * h:C * (h + 1)]
        return acc

    def expand(bg, col0):
        out = jnp.broadcast_to(bg[:, col0:col0 + 1], (C, W))
        for h in range(1, GDN_HEADS):
            out = jnp.where(lane_head == h, jnp.broadcast_to(bg[:, col0 + h:col0 + h + 1], (C, W)), out)
        return out

    def phase1(it, carry):
        probs = [(c, d) for c in range(U) for d in range(2)]
        each = lambda f, *ls: [f(*a) for a in zip(*ls)]
        per = lambda f: [f(c, d) for c, d in probs]
        rows = [pl.multiple_of((it * U + c) * C, C) for c in range(U)]
        q = [qkv_ref[0, pl.ds(r, C), 0:W] for r in rows]
        k = [qkv_ref[0, pl.ds(r, C), W:2 * W] for r in rows]
        v = [qkv_ref[0, pl.ds(r, C), 2 * W:3 * W] for r in rows]
        bg = [bg_ref[0, pl.ds(r, C), :] for r in rows]
        kbf = each(lambda x: x.astype(BF16), k)
        gram = each(lambda kb_, q_: _dot_nt(jnp.concatenate([kb_, q_.astype(BF16)], axis=0),
                                            jnp.concatenate([kb_] * GDN_HEADS, axis=0) * bmask), kbf, q)
        beta = per(lambda c, d: expand(bg[c], GDN_HEADS * d))
        g = per(lambda c, d: expand(bg[c], _N_BA + GDN_HEADS * d))
        dirs = [d for _, d in probs]
        chunk = [c for c, _ in probs]
        gc = each(lambda x, d: exact01(tri[d], x), g, dirs)
        g2 = each(lambda x, d: exact01(ones8, x * incl_t[d]), g, dirs)
        dlt = each(lambda a, b: a - jnp.broadcast_to(b[0:1], (C, W)), gc, g2)
        dec = each(lambda x, d: jnp.where(incl[d], jnp.exp(jnp.where(incl[d], x, 0.0)), 0.0), dlt, dirs)
        gcl = each(lambda x, d: x[last_row[d]:last_row[d] + 1], gc, dirs)
        egc = each(jnp.exp, gc)
        a_mat = each(lambda c, b, de: gram[c][0:C] * b * jnp.where(diag, 0.0, de), chunk, beta, dec)
        a_intra = each(lambda c, de: (gram[c][C:2 * C] * de).astype(BF16), chunk, dec)

        b_0 = each(lambda a: -(a * blk4), a_mat)
        p_0 = each(lambda b: b + mm(b, bd(b)), b_0)
        t_mat = each(lambda b, p: eye_lb + b + mm(p, bd(b)), b_0, p_0)
        for msk in lvl_masks:
            y = each(lambda a, x: mm(a * msk, bd(x)), a_mat, t_mat)
            t_mat = each(lambda x, y_: x - mm(x, bd(y_)), t_mat, y)

        u = each(lambda t, c, b: mm(t, bd(v[c] * b)), t_mat, chunk, beta)
        w = each(lambda t, c, b, e: mm(t, bd(k[c] * b * e)), t_mat, chunk, beta, egc)
        kd_t = each(lambda c, gl_, gc_: jnp.transpose(k[c] * jnp.exp(gl_ - gc_)).astype(BF16), chunk, gcl, gc)
        kwu = each(lambda kt, w_, u_: _dot(kt, jnp.concatenate([w_, u_], axis=1).astype(BF16)), kd_t, w, u)
        aw = each(lambda a, w_: _dot(a, bd(w_)), a_intra, w)
        au = each(lambda a, u_: _dot(a, bd(u_)), a_intra, u)
        for p, (c, d) in enumerate(probs):
            n = it * U + c
            mp_ref[d, n] = (-diag_blocks(kwu[p][:, 0:W] * bmask_f)).astype(BF16)
            cc_ref[d, n] = diag_blocks(kwu[p][:, W:2 * W] * bmask_f)
            qp_ref[d, n] = (q[c] * egc[p] - aw[p]).astype(BF16)
            gl_ref[d, n] = jnp.broadcast_to(jnp.exp(gcl[p]), (SUBLANE, W))
        for c in range(U):
            o_ref[0, pl.ds(rows[c], C), :] = au[2 * c] + au[2 * c + 1]
        return carry

    assert n_all % U == 0
    lax.fori_loop(0, n_all // U, phase1, 0)

    def phase2(i, carry):
        n = (jnp.where(i < n_ctx, n_lat + i, i - n_ctx), n_all - 1 - i)
        r = [_dot(jnp.concatenate([mp_ref[d, n[d]], qp_ref[d, n[d]]], axis=0), bd(carry[d])) for d in range(2)]
        new = []
        for d in range(2):
            r0 = pl.multiple_of(n[d] * C, C)
            o_ref[0, pl.ds(r0, C), :] += r[d][C:2 * C]
            new.append(gl_ref[d, n[d]][0:1] * carry[d] + r[d][0:C] + cc_ref[d, n[d]])
        return tuple(new)

    zero = jnp.zeros((C, W), F32)
    lax.fori_loop(0, n_all, phase2, (zero, zero))


def _gdn(qkv, bg, n_lat_tokens):
    B, LT, _ = qkv.shape
    n_all = LT // GDN_CHUNK
    n_lat = n_lat_tokens // GDN_CHUNK
    W = GDN_WIDTH
    return pl.pallas_call(
        functools.partial(_gdn_kernel, n_lat, n_all),
        grid=(B,),
        in_specs=[
            pl.BlockSpec((1, LT, 3 * W), lambda b: (b, 0, 0)),
            pl.BlockSpec((1, LT, LANE), lambda b: (b, 0, 0)),
        ],
        out_specs=pl.BlockSpec((1, LT, W), lambda b: (b, 0, 0)),
        out_shape=jax.ShapeDtypeStruct((B, LT, W), F32),
        scratch_shapes=[
            pltpu.VMEM((2, n_all, GDN_CHUNK, W), BF16),
            pltpu.VMEM((2, n_all, GDN_CHUNK, W), BF16),
            pltpu.VMEM((2, n_all, GDN_CHUNK, W), F32),
            pltpu.VMEM((2, n_all, SUBLANE, W), F32),
        ],
        compiler_params=_params("parallel"),
        name="gdn",
    )(qkv, bg)


def _merge_kernel(final, x_ref, oa_ref, og_ref, sg_ref, gt_ref, ong_ref, gsum_ref, wout_ref, fg_ref, o_ref):
    wa = oa_ref.shape[2]
    subs = [slice(TOK_TILE * s, TOK_TILE * (s + 1)) for s in range(oa_ref.shape[1])]
    ogs = [og_ref[0, sl, :] for sl in subs]
    ssq = [_group_sum(og * og, gsum_ref[...]) for og in ogs]
    ms = []
    for s, sl in enumerate(subs):
        ogn = ogs[s] * lax.rsqrt(ssq[s] * (1.0 / GDN_DV) + EPS) * ong_ref[...]
        ma = (jnp.transpose(oa_ref[0, s].astype(F32)) * sg_ref[0, sl, 0:wa].astype(F32)).astype(BF16)
        mg = (ogn * sg_ref[0, sl, wa:].astype(F32)).astype(BF16)
        ms.append((ma, mg))
    ys = [_dot(ma, wout_ref[0:wa, :]) + _dot(mg, wout_ref[wa:, :]) for ma, mg in ms]
    for s, sl in enumerate(subs):
        xn = x_ref[0, sl, :] + gt_ref[0] * ys[s]
        if final:
            xn = xn * lax.rsqrt(jnp.mean(xn * xn, axis=-1, keepdims=True) + EPS) * fg_ref[...]
        o_ref[0, sl, :] = xn


def _merge(x, o_attn, o_gdn, sg, gt, lw, gsum, final_g, row_start, final):
    B, L, D = x.shape
    sub = min(MERGE_SUB, L // TOK_TILE)
    rows = sub * TOK_TILE
    nt = L // rows
    t0 = row_start // rows
    full = lambda a: pl.BlockSpec(a.shape, lambda b, t: (0,) * a.ndim)
    return pl.pallas_call(
        functools.partial(_merge_kernel, final),
        grid=(B, nt),
        in_specs=[
            pl.BlockSpec((1, rows, D), lambda b, t: (b, t, 0)),
            pl.BlockSpec((1, sub, o_attn.shape[2], TOK_TILE), lambda b, t: (b, t, 0, 0)),
            pl.BlockSpec((1, rows, GDN_WIDTH), lambda b, t: (b, t0 + t, 0)),
            pl.BlockSpec((1, rows, D_MIX), lambda b, t: (b, t0 + t, 0)),
            pl.BlockSpec((1, 1, D), lambda b, t: (b, 0, 0)),
            full(lw["on_g"]), full(gsum), full(lw["w_out"]), full(final_g),
        ],
        out_specs=pl.BlockSpec((1, rows, D), lambda b, t: (b, t, 0)),
        out_shape=jax.ShapeDtypeStruct((B, L, D), F32),
        compiler_params=_params("parallel", "arbitrary"),
        name="merge",
    )(x, o_attn, o_gdn, sg, gt, lw["on_g"], gsum, lw["w_out"], final_g)


def _layer_weights(l, w_in, norm_g, mla_qn_g, mla_w_uq, mla_kvn_g, mla_w_ukv, gqa_qn_g, gqa_kn_g,
                   gdn_conv_w, gdn_a_log, gdn_dt_bias, gdn_on_g, w_out):
    D = D_MODEL
    w = w_in[l]
    splits = np.cumsum([0, MLA_Q_RANK, MLA_KV_RANK, MLA_ROPE, GQA_HEADS * GQA_DIM, GQA_KV_HEADS * GQA_DIM,
                        GQA_KV_HEADS * GQA_DIM, 3 * GDN_HEADS * GDN_DK, 2 * GDN_HEADS, 2 * GDN_HEADS, D_MIX])
    seg = [w[:, int(a):int(b)] for a, b in zip(splits[:-1], splits[1:])]
    cq, ckv, kr, gq, gk, gv, gdn, bb, aa, gate = seg
    z = lambda n: jnp.zeros((D, n), w.dtype)
    kr_p = jnp.concatenate([z(MLA_NOPE), kr, z(LANE - MLA_NOPE - MLA_ROPE)], axis=1)
    k0, k1 = gk[:, :GQA_DIM], gk[:, GQA_DIM:]
    gk2 = jnp.concatenate([k0, k1, k1, k0], axis=1)
    ba = jnp.concatenate([bb, aa, z(LANE - 2 * _N_BA)], axis=1)
    w_p = jnp.concatenate([cq, ckv, kr_p, gq, gk2, gdn, ba, gate], axis=1).astype(BF16)

    dqk = MLA_NOPE + MLA_ROPE
    uq = mla_w_uq[l].reshape(MLA_Q_RANK, MLA_HEADS, dqk)
    uq_p = jnp.concatenate([uq, jnp.zeros((MLA_Q_RANK, MLA_HEADS, LANE - dqk), uq.dtype)], axis=2)
    ukv = mla_w_ukv[l].reshape(MLA_KV_RANK, MLA_HEADS, MLA_NOPE + MLA_V)
    uk_p = jnp.concatenate([ukv[:, :, :MLA_NOPE],
                            jnp.zeros((MLA_KV_RANK, MLA_HEADS, LANE - MLA_NOPE), ukv.dtype)], axis=2)
    uv = ukv[:, :, MLA_NOPE:]
    return {
        "norm_g": norm_g[l].reshape(1, D),
        "w_in": w_p,
        "qn_g": mla_qn_g[l].reshape(1, -1),
        "w_uq": uq_p.reshape(MLA_Q_RANK, MLA_HEADS * LANE).astype(BF16),
        "kvn_g": mla_kvn_g[l].reshape(1, -1),
        "w_uk": uk_p.reshape(MLA_KV_RANK, MLA_HEADS * LANE).astype(BF16),
        "w_uv_t": uv.reshape(MLA_KV_RANK, MLA_HEADS * MLA_V).T.astype(BF16),
        "w_gv_t": gv.T.astype(BF16),
        "gq_g": (jnp.tile(gqa_qn_g[l], GQA_HEADS) * (GQA_SCALE * LOG2E)).reshape(1, -1),
        "gk_g": jnp.tile(gqa_kn_g[l], 2 * 2).reshape(1, -1),
        "conv_w": jnp.concatenate([gdn_conv_w[l], jnp.zeros((SUBLANE - GDN_CONV, _W_GDN), F32)], axis=0),
        "alog_x": jnp.concatenate([jnp.zeros((_N_BA,), F32), gdn_a_log[l].reshape(-1),
                                   jnp.zeros((LANE - 2 * _N_BA,), F32)]).reshape(1, LANE),
        "dtb_x": jnp.concatenate([jnp.zeros((_N_BA,), F32), gdn_dt_bias[l].reshape(-1),
                                  jnp.zeros((LANE - 2 * _N_BA,), F32)]).reshape(1, LANE),
        "on_g": jnp.tile(gdn_on_g[l], GDN_HEADS).reshape(1, -1),
        "w_out": w_out[l].astype(BF16),
    }


def _rope_tables(seq_len):
    LT = seq_len + CTX_LEN
    t = np.arange(seq_len)
    rows, cols = (t // GRID_W).astype(np.float64), (t % GRID_W).astype(np.float64)

    def fill(c, s, lane0, d, pos):
        half = d // 2
        inv = ROPE_BASE ** (-np.arange(0, d, 2, dtype=np.float64) / d)
        ang = pos[:, None] * inv[None, :]
        c[:seq_len, lane0:lane0 + half] = np.cos(ang)
        c[:seq_len, lane0 + half:lane0 + d] = np.cos(ang)
        s[:seq_len, lane0:lane0 + half] = -np.sin(ang)
        s[:seq_len, lane0 + half:lane0 + d] = np.sin(ang)

    ca, sa = np.ones((LT, LANE)), np.zeros((LT, LANE))
    fill(ca, sa, MLA_NOPE, MLA_ROPE // 2, rows)
    fill(ca, sa, MLA_NOPE + MLA_ROPE // 2, MLA_ROPE // 2, cols)
    cb, sb = np.ones((LT, LANE)), np.zeros((LT, LANE))
    for h0 in (0, GQA_DIM):
        fill(cb, sb, h0, GQA_DIM // 2, rows)
        fill(cb, sb, h0 + GQA_DIM // 2, GQA_DIM // 2, cols)
    lane = np.arange(3 * LANE)
    gsum = (lane[:, None] // GQA_DIM == lane[None, :] // GQA_DIM).astype(np.float32)
    return {"ca": jnp.asarray(ca, F32), "sa": jnp.asarray(sa, F32),
            "cb": jnp.asarray(cb, F32), "sb": jnp.asarray(sb, F32),
            "gsum": jnp.asarray(gsum, BF16)}


def kernel(x, c, ctx, c_ctx, w_ada, b_ada, norm_g, w_in, mla_qn_g, mla_w_uq, mla_kvn_g, mla_w_ukv,
           gqa_qn_g, gqa_kn_g, gdn_conv_w, gdn_a_log, gdn_dt_bias, gdn_on_g, w_out, final_g):
    B, L, D = x.shape
    assert D == D_MODEL and L % TOK_TILE == 0 and ctx.shape[1] == CTX_LEN
    tabs = _rope_tables(L)
    rows = -(-(B + 1) // SUBLANE) * SUBLANE
    cvec = jnp.concatenate([c, c_ctx[None, :], jnp.zeros((rows - B - 1, D), F32)], axis=0)
    mod = _adaln(cvec, w_ada, b_ada)
    fg = final_g.reshape(1, D)
    gs256 = tabs["gsum"][:GDN_WIDTH, :GDN_WIDTH]

    for l in range(DEPTH):
        lw = _layer_weights(l, w_in, norm_g, mla_qn_g, mla_w_uq, mla_kvn_g, mla_w_ukv, gqa_qn_g,
                            gqa_kn_g, gdn_conv_w, gdn_a_log, gdn_dt_bias, gdn_on_g, w_out)
        ml = mod[l, :B].reshape(B, 1, 3 * D)
        mc = mod[l, B].reshape(1, 3 * D)
        shl, scl, gtl = ml[..., :D], ml[..., D:2 * D], ml[..., 2 * D:]
        shc, scc, gtc = mc[:, :D], mc[:, D:2 * D], mc[:, 2 * D:]
        q, k, v_t, qkv, bg, sg = _inproj(x, ctx, shl, scl, shc, scc, lw, tabs)
        o_gdn = _gdn(qkv, bg, L)
        o_lat = _attention(q, k, v_t, 0, L, 0, L + CTX_LEN)
        last = l == DEPTH - 1
        x_new = _merge(x, o_lat, o_gdn, sg, gtl, lw, gs256, fg, 0, last)
        if not last:
            o_ctx = _attention(q, k, v_t, L, CTX_LEN, L, CTX_LEN)
            gtc_b = jnp.broadcast_to(gtc.reshape(1, 1, D), (B, 1, D))
            ctx = _merge(ctx, o_ctx, o_gdn, sg, gtc_b, lw, gs256, fg, L, False)
        x = x_new
    return x
```
